```python
import math
import jax
import jax.numpy as jnp
from jax import lax
import numpy as np

D_MODEL = 2048
BATCH = 1
SEQ = 16384
DEPTH = 1
DEC_BATCH = 32
DEC_SEQ = 64
PAST_LEN = 1024

CHUNK = 64
HG_HEADS = 8
HG_KEY_DIM = 128
HG_VAL_DIM = 128
HG_BLOCK = 16
DA_HEADS = 8
DA_QK_DIM = 64
DA_V_DIM = 2 * DA_QK_DIM
Q_BLOCK = 128
N_EXPERTS = 32
TOP_K = 4
D_EXPERT = D_MODEL
SWIGLU_LIMIT = 7.0
SWIGLU_ALPHA = 1.702
MOE_BLOCK = 256
RMS_EPS = 1e-6

HG_KEY_WIDTH = HG_HEADS * HG_KEY_DIM
HG_VAL_WIDTH = HG_HEADS * HG_VAL_DIM
DA_QK_WIDTH = DA_HEADS * 2 * DA_QK_DIM
DA_V_WIDTH = DA_HEADS * DA_V_DIM
MIX_WIDTH = HG_VAL_WIDTH + DA_V_WIDTH
IN_SIZES = (HG_KEY_WIDTH, HG_KEY_WIDTH, HG_VAL_WIDTH, HG_VAL_WIDTH, DA_QK_WIDTH, DA_QK_WIDTH, DA_V_WIDTH)
IN_COLS = sum(IN_SIZES)
IN_SPLITS = tuple(int(c) for c in np.cumsum(IN_SIZES)[:-1])

kernel_name = 'hymba_hgrn2_diffattn_moe_stream_step'


def rmsnorm(x, g):
    x32 = x.astype(jnp.float32)
    y = x32 * lax.rsqrt(jnp.mean(x32 * x32, axis=-1, keepdims=True) + RMS_EPS)
    return (y * g.astype(jnp.float32)).astype(x.dtype)


def hgrn2_scan(q, k, logf, v, s0):
    B, T, H, dk = q.shape
    dv = v.shape[-1]
    n_blk = -(-T // HG_BLOCK)
    pad = n_blk * HG_BLOCK - T

    def blocks(a):
        a = jnp.pad(a.astype(jnp.float32), ((0, 0), (0, pad), (0, 0), (0, 0)))
        return a.reshape(B, n_blk, HG_BLOCK, H, a.shape[-1]).transpose(1, 0, 3, 2, 4)

    causal = jnp.tril(jnp.ones((HG_BLOCK, HG_BLOCK), dtype=bool))

    def step(s, blk):
        qb, kb, gb, vb = blk
        b = jnp.cumsum(gb, axis=2)
        o_inter = jnp.einsum('bhtd,bhde->bhte', qb * jnp.exp(b), s)
        diff = b[:, :, :, None, :] - b[:, :, None, :, :]
        decay = jnp.exp(jnp.where(causal[:, :, None], diff, -jnp.inf))
        scores = jnp.einsum('bhtd,bhsd,bhtsd->bhts', qb, kb, decay)
        o_intra = jnp.einsum('bhts,bhse->bhte', scores, vb)
        b_last = b[:, :, -1:, :]
        s_new = (jnp.exp(b_last[:, :, 0, :, None]) * s
                 + jnp.einsum('bhsd,bhse->bhde', kb * jnp.exp(b_last - b), vb))
        return s_new, o_inter + o_intra

    s_T, o = lax.scan(step, s0.astype(jnp.float32), (blocks(q), blocks(k), blocks(logf), blocks(v)))
    o = o.transpose(1, 0, 3, 2, 4).reshape(B, n_blk * HG_BLOCK, H, dv)[:, :T]
    return o, s_T


def hgrn2_mixer(q_raw, f_raw, i_raw, g_raw, lb, norm_g, s0):
    B, T = q_raw.shape[:2]
    q = jax.nn.silu(q_raw).reshape(B, T, HG_HEADS, HG_KEY_DIM)
    f = lb + (1.0 - lb) * jax.nn.sigmoid(f_raw.astype(jnp.float32))
    logf = jnp.log(f).reshape(B, T, HG_HEADS, HG_KEY_DIM)
    k = (1.0 - f).reshape(B, T, HG_HEADS, HG_KEY_DIM)
    v = i_raw.reshape(B, T, HG_HEADS, HG_VAL_DIM)
    o, s_T = hgrn2_scan(q, k, logf, v, s0)
    gate = jax.nn.sigmoid(g_raw.astype(jnp.float32)).reshape(B, T, HG_HEADS, HG_VAL_DIM)
    o = rmsnorm(o, norm_g) * gate
    return o.reshape(B, T, HG_VAL_WIDTH).astype(q_raw.dtype), s_T


def diff_combine(s, v, lam):
    p = jax.nn.softmax(s, axis=-1)
    p = p[:, :, 0] - lam * p[:, :, 1]
    return jnp.einsum('bhqk,bkhe->bqhe', p, v.astype(jnp.float32))


def diff_attn_prompt(q, k, v, lam):
    B, S = q.shape[:2]
    n_qb = S // Q_BLOCK
    scale = DA_QK_DIM ** -0.5
    qb = q.reshape(B, n_qb, Q_BLOCK, DA_HEADS, 2, DA_QK_DIM).transpose(1, 0, 2, 3, 4, 5)
    k32 = k.astype(jnp.float32)
    key_chunk = jnp.arange(S) // CHUNK

    def one_block(args):
        q_blk, i = args
        q_chunk = (i * Q_BLOCK + jnp.arange(Q_BLOCK)) // CHUNK
        mask = key_chunk[None, :] <= q_chunk[:, None]
        s = jnp.einsum('bqhcd,bkhcd->bhcqk', q_blk.astype(jnp.float32), k32) * scale
        s = jnp.where(mask, s, -jnp.inf)
        return diff_combine(s, v, lam)

    o = lax.map(one_block, (qb, jnp.arange(n_qb)))
    return o.transpose(1, 0, 2, 3, 4).reshape(B, S, DA_HEADS, DA_V_DIM)


def diff_attn_sample(q, k_new, v_new, cache_k, cache_v, lam):
    scale = DA_QK_DIM ** -0.5
    k = jnp.concatenate([cache_k.astype(k_new.dtype), k_new], axis=1)
    v = jnp.concatenate([cache_v.astype(v_new.dtype), v_new], axis=1)
    s = jnp.einsum('bqhcd,bkhcd->bhcqk', q.astype(jnp.float32), k.astype(jnp.float32)) * scale
    return diff_combine(s, v, lam)


def moe_ffn(h, w_router, b_router, w_gu, b_gu, w_down, b_down):
    n_tok = h.shape[0]
    logits = h.astype(jnp.float32) @ w_router.astype(jnp.float32) + b_router.astype(jnp.float32)
    top_val, top_idx = lax.top_k(logits, TOP_K)
    gates = jax.nn.softmax(top_val, axis=-1)
    n_assign = n_tok * TOP_K
    n_rows = -(-(n_assign + N_EXPERTS * (MOE_BLOCK - 1)) // MOE_BLOCK) * MOE_BLOCK
    n_blocks = n_rows // MOE_BLOCK
    flat_e = top_idx.reshape(-1)
    flat_tok = jnp.repeat(jnp.arange(n_tok, dtype=jnp.int32), TOP_K)
    flat_gate = gates.reshape(-1)
    order = jnp.argsort(flat_e)
    e_sorted = flat_e[order]
    counts = jnp.bincount(flat_e, length=N_EXPERTS)
    padded = (counts + MOE_BLOCK - 1) // MOE_BLOCK * MOE_BLOCK
    start = jnp.cumsum(counts) - counts
    pend = jnp.cumsum(padded)
    pstart = pend - padded
    dest = pstart[e_sorted] + (jnp.arange(n_assign) - start[e_sorted])
    row_tok = jnp.full((n_rows,), n_tok, dtype=jnp.int32).at[dest].set(flat_tok[order])
    row_gate = jnp.zeros((n_rows,), dtype=h.dtype).at[dest].set(flat_gate[order].astype(h.dtype))
    block_expert = jnp.minimum(
        jnp.searchsorted(pend, jnp.arange(n_blocks) * MOE_BLOCK, side='right'), N_EXPERTS - 1)
    h_pad = jnp.concatenate([h, jnp.zeros((1, h.shape[1]), h.dtype)], axis=0)
    xb = h_pad[row_tok].reshape(n_blocks, MOE_BLOCK, h.shape[1])

    def expert_block(args):
        xe, e = args
        gu = xe @ w_gu[e] + b_gu[e]
        gate, up = gu[:, :D_EXPERT], gu[:, D_EXPERT:]
        gate = jnp.minimum(gate, SWIGLU_LIMIT)
        up = jnp.clip(up, -SWIGLU_LIMIT, SWIGLU_LIMIT)
        glu = gate * jax.nn.sigmoid(SWIGLU_ALPHA * gate)
        return ((up + 1.0) * glu) @ w_down[e] + b_down[e]

    yb = lax.map(expert_block, (xb, block_expert)).reshape(n_rows, h.shape[1])
    y = jax.ops.segment_sum(yb * row_gate[:, None], row_tok, num_segments=n_tok + 1)
    return y[:n_tok]


def setup_inputs(seed: int = 0) -> dict:
    key = jax.random.key(seed)
    ks = jax.random.split(key, 21)

    def nrm(k, shape, scale):
        return jax.random.normal(k, shape, jnp.float32) * scale

    return {
        'x_prompt': nrm(ks[0], (BATCH, SEQ, D_MODEL), 1.0),
        'x_sample': nrm(ks[1], (DEC_BATCH, DEC_SEQ, D_MODEL), 1.0),
        'cache_k': nrm(ks[2], (DEPTH, DEC_BATCH, PAST_LEN, DA_HEADS, 2, DA_QK_DIM), 1.0),
        'cache_v': nrm(ks[3], (DEPTH, DEC_BATCH, PAST_LEN, DA_HEADS, DA_V_DIM), 1.0),
        'state_hgrn': nrm(ks[4], (DEPTH, DEC_BATCH, HG_HEADS, HG_KEY_DIM, HG_VAL_DIM), 0.5),
        'attn_norm_g': 1.0 + nrm(ks[5], (DEPTH, D_MODEL), 0.02),
        'w_in': nrm(ks[6], (DEPTH, D_MODEL, IN_COLS), D_MODEL ** -0.5),
        'hg_lb_logits': nrm(ks[7], (DEPTH + 1, HG_KEY_WIDTH), 0.5),
        'hg_norm_g': 1.0 + nrm(ks[8], (DEPTH, HG_VAL_DIM), 0.02),
        'da_q_norm_g': 1.0 + nrm(ks[9], (DEPTH, DA_QK_DIM), 0.02),
        'da_k_norm_g': 1.0 + nrm(ks[10], (DEPTH, DA_QK_DIM), 0.02),
        'da_lambda': nrm(ks[11], (DEPTH, 4, DA_QK_DIM), 0.1),
        'da_subln_g': 1.0 + nrm(ks[12], (DEPTH, DA_V_DIM), 0.02),
        'w_out': nrm(ks[13], (DEPTH, MIX_WIDTH, D_MODEL), MIX_WIDTH ** -0.5),
        'ffn_norm_g': 1.0 + nrm(ks[14], (DEPTH, D_MODEL), 0.02),
        'w_router': nrm(ks[15], (DEPTH, D_MODEL, N_EXPERTS), D_MODEL ** -0.5),
        'b_router': nrm(ks[16], (DEPTH, N_EXPERTS), 0.01),
        'w_gu': nrm(ks[17], (DEPTH, N_EXPERTS, D_MODEL, 2 * D_EXPERT), D_MODEL ** -0.5),
        'b_gu': nrm(ks[18], (DEPTH, N_EXPERTS, 2 * D_EXPERT), 0.02),
        'w_down': nrm(ks[19], (DEPTH, N_EXPERTS, D_EXPERT, D_MODEL), D_EXPERT ** -0.5),
        'b_down': nrm(ks[20], (DEPTH, N_EXPERTS, D_MODEL), 0.02),
    }


def reference(x_prompt, x_sample, cache_k, cache_v, state_hgrn, attn_norm_g, w_in, hg_lb_logits,
              hg_norm_g, da_q_norm_g, da_k_norm_g, da_lambda, da_subln_g, w_out, ffn_norm_g,
              w_router, b_router, w_gu, b_gu, w_down, b_down):
    Bp, S, D = x_prompt.shape
    Bs, T, _ = x_sample.shape
    n_p = Bp * S
    n_tot = n_p + Bs * T
    lb_all = jnp.cumsum(jax.nn.softmax(hg_lb_logits.astype(jnp.float32), axis=0), axis=0)
    x = jnp.concatenate([x_prompt.reshape(n_p, D), x_sample.reshape(Bs * T, D)], axis=0)
    k_p_all, v_p_all, s_p_all, k_s_all, v_s_all, s_s_all = [], [], [], [], [], []
    for l in range(DEPTH):
        lam_init = 0.8 - 0.6 * math.exp(-0.3 * l)
        h = rmsnorm(x, attn_norm_g[l])
        proj = h @ w_in[l]
        hq, hf, hi, hg, dq, dk, dv = jnp.split(proj, IN_SPLITS, axis=-1)

        hg_p, s_p = hgrn2_mixer(hq[:n_p].reshape(Bp, S, -1), hf[:n_p].reshape(Bp, S, -1),
                                hi[:n_p].reshape(Bp, S, -1), hg[:n_p].reshape(Bp, S, -1),
                                lb_all[l], hg_norm_g[l],
                                jnp.zeros((Bp, HG_HEADS, HG_KEY_DIM, HG_VAL_DIM), jnp.float32))
        hg_s, s_s = hgrn2_mixer(hq[n_p:].reshape(Bs, T, -1), hf[n_p:].reshape(Bs, T, -1),
                                hi[n_p:].reshape(Bs, T, -1), hg[n_p:].reshape(Bs, T, -1),
                                lb_all[l], hg_norm_g[l], state_hgrn[l])

        q_n = rmsnorm(dq.reshape(n_tot, DA_HEADS, 2, DA_QK_DIM), da_q_norm_g[l])
        k_n = rmsnorm(dk.reshape(n_tot, DA_HEADS, 2, DA_QK_DIM), da_k_norm_g[l])
        v_n = dv.reshape(n_tot, DA_HEADS, DA_V_DIM)
        lp = da_lambda[l].astype(jnp.float32)
        lam = jnp.exp(jnp.sum(lp[0] * lp[1])) - jnp.exp(jnp.sum(lp[2] * lp[3])) + lam_init
        k_p = k_n[:n_p].reshape(Bp, S, DA_HEADS, 2, DA_QK_DIM)
        v_p = v_n[:n_p].reshape(Bp, S, DA_HEADS, DA_V_DIM)
        k_s = k_n[n_p:].reshape(Bs, T, DA_HEADS, 2, DA_QK_DIM)
        v_s = v_n[n_p:].reshape(Bs, T, DA_HEADS, DA_V_DIM)
        da_p = diff_attn_prompt(q_n[:n_p].reshape(Bp, S, DA_HEADS, 2, DA_QK_DIM), k_p, v_p, lam)
        da_s = diff_attn_sample(q_n[n_p:].reshape(Bs, T, DA_HEADS, 2, DA_QK_DIM), k_s, v_s,
                                cache_k[l], cache_v[l], lam)
        da = jnp.concatenate([da_p.reshape(n_p, DA_HEADS, DA_V_DIM),
                              da_s.reshape(Bs * T, DA_HEADS, DA_V_DIM)], axis=0)
        da = (rmsnorm(da, da_subln_g[l]) * (1.0 - lam_init)).astype(x.dtype).reshape(n_tot, DA_V_WIDTH)

        hg_all = jnp.concatenate([hg_p.reshape(n_p, HG_VAL_WIDTH), hg_s.reshape(Bs * T, HG_VAL_WIDTH)], axis=0)
        mix = jnp.concatenate([hg_all, da], axis=-1)
        x = x + (mix @ w_out[l]).astype(x.dtype)

        x = x + moe_ffn(rmsnorm(x, ffn_norm_g[l]), w_router[l], b_router[l], w_gu[l], b_gu[l],
                        w_down[l], b_down[l]).astype(x.dtype)

        k_p_all.append(k_p)
        v_p_all.append(v_p)
        s_p_all.append(s_p)
        k_s_all.append(k_s)
        v_s_all.append(v_s)
        s_s_all.append(s_s)

    y_prompt = x[:n_p].reshape(Bp, S, D)
    y_sample = x[n_p:].reshape(Bs, T, D)
    return (y_prompt, y_sample, jnp.stack(k_p_all), jnp.stack(v_p_all), jnp.stack(s_p_all),
            jnp.stack(k_s_all), jnp.stack(v_s_all), jnp.stack(s_s_all))
```

```python
import functools
import math

import jax
import jax.numpy as jnp
from jax import lax
from jax.experimental import pallas as pl
from jax.experimental.pallas import tpu as pltpu

F32 = jnp.float32
BF16 = jnp.bfloat16
I32 = jnp.int32

RMS_EPS = 1e-6
CHUNK = 64
HG_HEADS = 8
HG_DIM = 128
DA_HEADS = 8
DA_QK_DIM = 64
DA_V_DIM = 128
HEAD_WIDTH = 1024
N_EXPERTS = 32
TOP_K = 4
SWIGLU_LIMIT = 7.0
SWIGLU_ALPHA = 1.702
LANES = 128
HG_TILE = 64
HG_SUB = 16
EXP_CLAMP = 80.0
VMEM_LIMIT = 56 * 1024 * 1024
PROJ_TM, PROJ_TN = 512, 512
DA_TM = 256
ATTN_TILE = 512
MIX_TM = 256
DISPATCH_TM = 256
MOE_BM, MOE_TH = 512, 256
COMBINE_TM = 128


def _cparams(sem, vmem=VMEM_LIMIT):
    return pltpu.CompilerParams(dimension_semantics=sem, vmem_limit_bytes=vmem)


def _dot(a, b):
    return jnp.dot(a, b, preferred_element_type=F32)


def _dot_nt(a, b):
    return lax.dot_general(a, b, (((1,), (1,)), ((), ())), preferred_element_type=F32)


def _split_bf16(x):
    hi = x.astype(BF16)
    lo = (x - hi.astype(F32)).astype(BF16)
    return hi, lo


def _sigmoid(x):
    return 1.0 / (1.0 + jnp.exp(-x))


def _group_sum(sq, gmat):
    outs = []
    for c in range(sq.shape[1] // 256):
        hi, lo = _split_bf16(sq[:, c * 256:(c + 1) * 256])
        outs.append(_dot(hi, gmat) + _dot(lo, gmat))
    return jnp.concatenate(outs, axis=1)


def _proj_kernel(x_ref, g_ref, w_ref, o_ref, h_scr):
    @pl.when(pl.program_id(1) == 0)
    def _():
        x = x_ref[...]
        ms = jnp.mean(x * x, axis=-1, keepdims=True)
        h_scr[...] = ((x * lax.rsqrt(ms + RMS_EPS)) * g_ref[...]).astype(BF16)

    o_ref[...] = _dot(h_scr[...], w_ref[...])


def _proj(x, g, w_bf16, tm, tn):
    n, d = x.shape
    cols = w_bf16.shape[1]
    return pl.pallas_call(
        _proj_kernel,
        grid=(n // tm, cols // tn),
        in_specs=[pl.BlockSpec((tm, d), lambda i, j: (i, 0)),
                  pl.BlockSpec((1, d), lambda i, j: (0, 0)),
                  pl.BlockSpec((d, tn), lambda i, j: (0, j))],
        out_specs=pl.BlockSpec((tm, tn), lambda i, j: (i, j)),
        out_shape=jax.ShapeDtypeStruct((n, cols), F32),
        scratch_shapes=[pltpu.VMEM((tm, d), BF16)],
        compiler_params=_cparams(("arbitrary", "arbitrary")),
        name="proj",
    )(x, g, w_bf16)


def _hgrn_kernel(n_prompt_tiles, q_ref, f_ref, i_ref, g_ref, lb_ref, ng_ref, s0_ref,
                 o_ref, sout_ref, st_scr):
    t = pl.program_id(0)
    starts = jnp.logical_or(t == 0, t >= n_prompt_tiles)
    ends = t >= n_prompt_tiles - 1

    @pl.when(starts)
    def _():
        for h in range(HG_HEADS):
            st_scr[h] = s0_ref[0, h].T

    lb = lb_ref[...]
    q = q_ref[...]
    q = q * _sigmoid(q)
    f = lb + (1.0 - lb) * _sigmoid(f_ref[...])
    logf = jnp.log(f)
    k = 1.0 - f
    v = i_ref[...]
    gate = _sigmoid(g_ref[...])

    row = lax.broadcasted_iota(I32, (HG_TILE, HG_TILE), 0)
    col = lax.broadcasted_iota(I32, (HG_TILE, HG_TILE), 1)
    causal = col <= row
    tri = causal.astype(BF16)
    hi, lo = _split_bf16(logf)
    b = _dot(tri, hi) + _dot(tri, lo)
    b_last = b[HG_TILE - 1:HG_TILE, :]
    q_dec = q * jnp.exp(b)
    k_dec = k * jnp.exp(b_last - b)
    ng = ng_ref[...]

    outs = []
    for h in range(HG_HEADS):
        sl = slice(h * HG_DIM, (h + 1) * HG_DIM)
        st = st_scr[h]
        bh = b[:, sl]
        qh = q[:, sl]
        kh = k[:, sl]
        vh = v[:, sl].astype(BF16)
        o = _dot_nt(q_dec[:, sl].astype(BF16), st.astype(BF16))
        rows = []
        for i in range(HG_TILE // HG_SUB):
            r0 = i * HG_SUB
            base = bh[r0 - 1:r0, :] if i else jnp.zeros((1, HG_DIM), F32)
            qi = qh[r0:r0 + HG_SUB, :] * jnp.exp(bh[r0:r0 + HG_SUB, :] - base)
            ki = kh * jnp.exp(jnp.minimum(base - bh, EXP_CLAMP))
            rows.append(_dot_nt(qi.astype(BF16), ki.astype(BF16)))
        scores = jnp.where(causal, jnp.concatenate(rows, axis=0), 0.0)
        o = o + _dot(scores.astype(BF16), vh)
        st_new = st * jnp.exp(b_last[:, sl]) + _dot(v[:, sl].T.astype(BF16),
                                                    k_dec[:, sl].astype(BF16))
        st_scr[h] = st_new
        ms = jnp.mean(o * o, axis=-1, keepdims=True)
        outs.append((o * lax.rsqrt(ms + RMS_EPS)) * ng)
    o_ref[...] = (jnp.concatenate(outs, axis=1) * gate).astype(o_ref.dtype)

    @pl.when(ends)
    def _():
        for h in range(HG_HEADS):
            sout_ref[0, h] = st_scr[h].T


def _hgrn(proj, lb, norm_g, s0_all, n_prompt_tiles):
    n = proj.shape[0]
    n_tiles = n // HG_TILE
    seq = lambda t: (jnp.maximum(t - (n_prompt_tiles - 1), 0), 0, 0, 0)
    col_spec = lambda c: pl.BlockSpec((HG_TILE, HEAD_WIDTH), lambda t, c=c: (t, c))
    return pl.pallas_call(
        functools.partial(_hgrn_kernel, n_prompt_tiles),
        grid=(n_tiles,),
        in_specs=[col_spec(0), col_spec(1), col_spec(2), col_spec(3),
                  pl.BlockSpec((1, HEAD_WIDTH), lambda t: (0, 0)),
                  pl.BlockSpec((1, HG_DIM), lambda t: (0, 0)),
                  pl.BlockSpec((1, HG_HEADS, HG_DIM, HG_DIM), seq)],
        out_specs=[pl.BlockSpec((HG_TILE, HEAD_WIDTH), lambda t: (t, 0)),
                   pl.BlockSpec((1, HG_HEADS, HG_DIM, HG_DIM), seq)],
        out_shape=[jax.ShapeDtypeStruct((n, HEAD_WIDTH), BF16),
                   jax.ShapeDtypeStruct(s0_all.shape, F32)],
        scratch_shapes=[pltpu.VMEM((HG_HEADS, HG_DIM, HG_DIM), F32)],
        compiler_params=_cparams(("arbitrary",)),
        name="hgrn",
    )(proj, proj, proj, proj, lb, norm_g, s0_all)


def _da_prep_kernel(n_prompt_tiles, dq_ref, dk_ref, dv_ref, gq_ref, gk_ref, gmat_ref,
                    qlo_ref, qhi_ref, kb_ref, vt_ref, kp_ref, ks_ref, vp_ref, vs_ref):
    i = pl.program_id(0)
    gmat = gmat_ref[...]
    inv = 1.0 / DA_QK_DIM

    dq = dq_ref[...]
    qn = dq * lax.rsqrt(_group_sum(dq * dq, gmat) * inv + RMS_EPS) * gq_ref[...]
    qn = qn * (DA_QK_DIM ** -0.5)
    lane = lax.broadcasted_iota(I32, qn.shape, 1)
    first = (lane & DA_QK_DIM) == 0
    qlo_ref[...] = jnp.where(first, qn, 0.0).astype(BF16)
    qhi_ref[...] = jnp.where(first, 0.0, qn).astype(BF16)

    dk = dk_ref[...]
    kn = dk * lax.rsqrt(_group_sum(dk * dk, gmat) * inv + RMS_EPS) * gk_ref[...]
    kb_ref[...] = kn.astype(BF16)
    v = dv_ref[...]
    vt_ref[...] = v.T.astype(BF16)

    @pl.when(i < n_prompt_tiles)
    def _():
        kp_ref[...] = kn
        vp_ref[...] = v

    @pl.when(i >= n_prompt_tiles)
    def _():
        ks_ref[...] = kn
        vs_ref[...] = v


def _da_prep(proj, gq, gk, gmat, n_prompt, tm):
    n = proj.shape[0]
    npt = n_prompt // tm
    full = lambda: pl.BlockSpec((tm, HEAD_WIDTH), lambda i: (i, 0))
    p_spec = lambda: pl.BlockSpec((tm, HEAD_WIDTH), lambda i: (jnp.minimum(i, npt - 1), 0))
    s_spec = lambda: pl.BlockSpec((tm, HEAD_WIDTH), lambda i: (jnp.maximum(i - npt, 0), 0))
    vec = lambda: pl.BlockSpec((1, HEAD_WIDTH), lambda i: (0, 0))
    return pl.pallas_call(
        functools.partial(_da_prep_kernel, npt),
        grid=(n // tm,),
        in_specs=[pl.BlockSpec((tm, HEAD_WIDTH), lambda i: (i, 4)),
                  pl.BlockSpec((tm, HEAD_WIDTH), lambda i: (i, 5)),
                  pl.BlockSpec((tm, HEAD_WIDTH), lambda i: (i, 6)),
                  vec(), vec(),
                  pl.BlockSpec((256, 256), lambda i: (0, 0))],
        out_specs=[full(), full(), full(),
                   pl.BlockSpec((HEAD_WIDTH, tm), lambda i: (0, i)),
                   p_spec(), s_spec(), p_spec(), s_spec()],
        out_shape=[jax.ShapeDtypeStruct((n, HEAD_WIDTH), BF16),
                   jax.ShapeDtypeStruct((n, HEAD_WIDTH), BF16),
                   jax.ShapeDtypeStruct((n, HEAD_WIDTH), BF16),
                   jax.ShapeDtypeStruct((HEAD_WIDTH, n), BF16),
                   jax.ShapeDtypeStruct((n_prompt, HEAD_WIDTH), F32),
                   jax.ShapeDtypeStruct((n - n_prompt, HEAD_WIDTH), F32),
                   jax.ShapeDtypeStruct((n_prompt, HEAD_WIDTH), F32),
                   jax.ShapeDtypeStruct((n - n_prompt, HEAD_WIDTH), F32)],
        compiler_params=_cparams(("arbitrary",)),
        name="da_prep",
    )(proj, proj, proj, gq, gk, gmat)


def _lambda_of(lp, lam_init):
    a = jnp.sum(lp[0:1, :] * lp[1:2, :], axis=-1, keepdims=True)
    b = jnp.sum(lp[2:3, :] * lp[3:4, :], axis=-1, keepdims=True)
    return jnp.exp(a) - jnp.exp(b) + lam_init


def _subln(o, g, lam_init):
    ms = jnp.mean(o * o, axis=-1, keepdims=True)
    return (o * lax.rsqrt(ms + RMS_EPS)) * g * (1.0 - lam_init)


def _attn_p_kernel(lam_init, tile, qi_ref, kj_ref, qlo_ref, qhi_ref, k_ref, vt_ref, lp_ref,
                   sg_ref, o_ref, m_scr, l_scr, acc_scr):
    s = pl.program_id(0)
    qi = qi_ref[s]
    kj = kj_ref[s]

    @pl.when(kj == 0)
    def _():
        m_scr[...] = jnp.full(m_scr.shape, -jnp.inf, F32)
        l_scr[...] = jnp.zeros(l_scr.shape, F32)
        acc_scr[...] = jnp.zeros(acc_scr.shape, F32)

    shift = CHUNK.bit_length() - 1
    key_chunk = lax.broadcasted_iota(I32, (tile, tile), 0) >> shift
    qry_chunk = lax.broadcasted_iota(I32, (tile, tile), 1) >> shift
    visible = jnp.logical_or(key_chunk <= qry_chunk, kj < qi)

    for h in range(DA_HEADS):
        sl = slice(h * DA_V_DIM, (h + 1) * DA_V_DIM)
        kh = k_ref[:, sl]
        vth = vt_ref[sl, :]
        for c, q_ref in enumerate((qlo_ref, qhi_ref)):
            r = 2 * h + c
            sc = _dot_nt(kh, q_ref[:, sl])
            sc = jnp.where(visible, sc, -jnp.inf)
            m_prev = m_scr[r:r + 1, :]
            m_new = jnp.maximum(m_prev, jnp.max(sc, axis=0, keepdims=True))
            alpha = jnp.exp(m_prev - m_new)
            e = jnp.exp(sc - m_new)
            l_scr[r:r + 1, :] = alpha * l_scr[r:r + 1, :] + jnp.sum(e, axis=0, keepdims=True)
            acc_scr[r] = alpha * acc_scr[r] + _dot(vth, e.astype(BF16))
            m_scr[r:r + 1, :] = m_new

    @pl.when(kj == qi)
    def _():
        lam = _lambda_of(lp_ref[...], lam_init)
        sg = sg_ref[...]
        outs = []
        for h in range(DA_HEADS):
            a0 = acc_scr[2 * h] / l_scr[2 * h:2 * h + 1, :]
            a1 = acc_scr[2 * h + 1] / l_scr[2 * h + 1:2 * h + 2, :]
            outs.append(_subln((a0 - lam * a1).T, sg, lam_init))
        o_ref[...] = jnp.concatenate(outs, axis=1).astype(o_ref.dtype)


def _attn_prompt(qlo, qhi, kb, vt, lp, sg, n_prompt, lam_init, tile):
    nq = n_prompt // tile
    qi_l, kj_l = [], []
    for i in range(nq):
        for j in range(i + 1):
            qi_l.append(i)
            kj_l.append(j)
    qi = jnp.asarray(qi_l, I32)
    kj = jnp.asarray(kj_l, I32)
    grid_spec = pltpu.PrefetchScalarGridSpec(
        num_scalar_prefetch=2,
        grid=(len(qi_l),),
        in_specs=[pl.BlockSpec((tile, HEAD_WIDTH), lambda s, qi, kj: (qi[s], 0)),
                  pl.BlockSpec((tile, HEAD_WIDTH), lambda s, qi, kj: (qi[s], 0)),
                  pl.BlockSpec((tile, HEAD_WIDTH), lambda s, qi, kj: (kj[s], 0)),
                  pl.BlockSpec((HEAD_WIDTH, tile), lambda s, qi, kj: (0, kj[s])),
                  pl.BlockSpec((4, DA_QK_DIM), lambda s, qi, kj: (0, 0)),
                  pl.BlockSpec((1, DA_V_DIM), lambda s, qi, kj: (0, 0))],
        out_specs=pl.BlockSpec((tile, HEAD_WIDTH), lambda s, qi, kj: (qi[s], 0)),
        scratch_shapes=[pltpu.VMEM((2 * DA_HEADS, tile), F32),
                        pltpu.VMEM((2 * DA_HEADS, tile), F32),
                        pltpu.VMEM((2 * DA_HEADS, DA_V_DIM, tile), F32)])
    return pl.pallas_call(
        functools.partial(_attn_p_kernel, lam_init, tile),
        grid_spec=grid_spec,
        out_shape=jax.ShapeDtypeStruct((n_prompt, HEAD_WIDTH), BF16),
        compiler_params=_cparams(("arbitrary",)),
        name="attn_p",
    )(qi, kj, qlo, qhi, kb, vt, lp, sg)


def _attn_s_kernel(lam_init, qlo_ref, qhi_ref, kn_ref, vn_ref, ck_ref, cv_ref, lp_ref, sg_ref,
                   o_ref):
    lam =_lambda_of(lp_ref[...], lam_init)
    sg = sg_ref[...]
    outs = []
    for h in range(DA_HEADS):
        sl = slice(h * DA_V_DIM, (h + 1) * DA_V_DIM)
        kc = ck_ref[0, :, sl].astype(BF16)
        kn = kn_ref[:, sl].astype(BF16)
        vc = cv_ref[0, :, sl].astype(BF16)
        vn = vn_ref[:, sl].astype(BF16)
        acc = None
        for c, q_ref in enumerate((qlo_ref, qhi_ref)):
            qh = q_ref[:, sl]
            s_c = _dot_nt(qh, kc)
            s_n = _dot_nt(qh, kn)
            m = jnp.maximum(jnp.max(s_c, axis=-1, keepdims=True),
                            jnp.max(s_n, axis=-1, keepdims=True))
            e_c = jnp.exp(s_c - m)
            e_n = jnp.exp(s_n - m)
            l = jnp.sum(e_c, axis=-1, keepdims=True) + jnp.sum(e_n, axis=-1, keepdims=True)
            part = (_dot(e_c.astype(BF16), vc) + _dot(e_n.astype(BF16), vn)) / l
            acc = part if c == 0 else acc - lam * part
        outs.append(_subln(acc, sg, lam_init))
    o_ref[...] = jnp.concatenate(outs, axis=1).astype(o_ref.dtype)


def _attn_sample(qlo, qhi, ks, vs, cache_k, cache_v, lp, sg, n_prompt, lam_init):
    bs, past = cache_k.shape[0], cache_k.shape[1]
    t = ks.shape[0] // bs
    off = n_prompt // t
    return pl.pallas_call(
        functools.partial(_attn_s_kernel, lam_init),
        grid=(bs,),
        in_specs=[pl.BlockSpec((t, HEAD_WIDTH), lambda b: (off + b, 0)),
                  pl.BlockSpec((t, HEAD_WIDTH), lambda b: (off + b, 0)),
                  pl.BlockSpec((t, HEAD_WIDTH), lambda b: (b, 0)),
                  pl.BlockSpec((t, HEAD_WIDTH), lambda b: (b, 0)),
                  pl.BlockSpec((1, past, HEAD_WIDTH), lambda b: (b, 0, 0)),
                  pl.BlockSpec((1, past, HEAD_WIDTH), lambda b: (b, 0, 0)),
                  pl.BlockSpec((4, DA_QK_DIM), lambda b: (0, 0)),
                  pl.BlockSpec((1, DA_V_DIM), lambda b: (0, 0))],
        out_specs=pl.BlockSpec((t, HEAD_WIDTH), lambda b: (b, 0)),
        out_shape=jax.ShapeDtypeStruct((bs * t, HEAD_WIDTH), BF16),
        compiler_params=_cparams(("arbitrary",)),
        name="attn_s",
    )(qlo, qhi, ks, vs, cache_k, cache_v, lp, sg)


def _mix_kernel(n_prompt_tiles, hg_ref, dap_ref, das_ref, x_ref, wo_ref, g2_ref, wr_ref, br_ref,
                x1_ref, h2_ref, route_ref, gates_ref, cnt_ref, base_scr):
    i = pl.program_id(0)

    @pl.when(i == 0)
    def _():
        base_scr[...] = jnp.zeros(base_scr.shape, F32)

    tm = x_ref.shape[0]
    da = jnp.where(i < n_prompt_tiles, dap_ref[...], das_ref[...])
    x1 = x_ref[...] + _dot(hg_ref[...], wo_ref[0:HEAD_WIDTH, :]) \
        + _dot(da, wo_ref[HEAD_WIDTH:2 * HEAD_WIDTH, :])
    x1_ref[...] = x1
    ms = jnp.mean(x1 * x1, axis=-1, keepdims=True)
    h2 = (x1 * lax.rsqrt(ms + RMS_EPS)) * g2_ref[...]
    h2_ref[...] = h2

    h_hi, h_lo = _split_bf16(h2)
    w_hi, w_lo = _split_bf16(wr_ref[...])
    logits = _dot(h_hi, w_hi) + _dot(h_lo, w_hi) + _dot(h_hi, w_lo) + br_ref[...]
    lane = lax.broadcasted_iota(I32, logits.shape, 1)
    lanef = lane.astype(F32)
    lg = jnp.where(lane < N_EXPERTS, logits, -jnp.inf)

    sels, vals, idxs = [], [], []
    for _ in range(TOP_K):
        mx = jnp.max(lg, axis=-1, keepdims=True)
        idx = jnp.min(jnp.where(lg == mx, lanef, float(LANES)), axis=-1, keepdims=True)
        sel = lanef == idx
        lg = jnp.where(sel, -jnp.inf, lg)
        sels.append(sel)
        vals.append(mx)
        idxs.append(idx)
    es = [jnp.exp(v - vals[0]) for v in vals]
    den = es[0] + es[1] + es[2] + es[3]

    onehot = jnp.zeros(logits.shape, F32)
    for sel in sels:
        onehot = onehot + sel.astype(F32)
    row = lax.broadcasted_iota(I32, (tm, tm), 0)
    col = lax.broadcasted_iota(I32, (tm, tm), 1)
    strict = (col < row).astype(BF16)
    before = _dot(strict, onehot.astype(BF16)) + base_scr[0:1, :]
    base_scr[0:1, :] = base_scr[0:1, :] + jnp.sum(onehot, axis=0, keepdims=True)

    route = jnp.zeros(logits.shape, F32)
    gates = jnp.zeros(logits.shape, F32)
    for r in range(TOP_K):
        rank = jnp.sum(jnp.where(sels[r], before, 0.0), axis=-1, keepdims=True)
        route = jnp.where(lane == r, idxs[r], route)
        route = jnp.where(lane == TOP_K + r, rank, route)
        gates = jnp.where(lane == r, es[r] / den, gates)
    route_ref[...] = route.astype(I32)
    gates_ref[...] = gates
    cnt_ref[...] = jnp.broadcast_to(base_scr[0:1, :], cnt_ref.shape).astype(I32)


def _mix(hg, da_p, da_s, x, wo_bf16, g2, wr_pad, br_pad, tm):
    n, d = x.shape
    npt = da_p.shape[0] // tm
    row = lambda w: pl.BlockSpec((tm, w), lambda i: (i, 0))
    const = lambda shp: pl.BlockSpec(shp, lambda i: (0, 0))
    return pl.pallas_call(
        functools.partial(_mix_kernel, npt),
        grid=(n // tm,),
        in_specs=[row(HEAD_WIDTH),
                  pl.BlockSpec((tm, HEAD_WIDTH), lambda i: (jnp.minimum(i, npt - 1), 0)),
                  pl.BlockSpec((tm, HEAD_WIDTH), lambda i: (jnp.maximum(i - npt, 0), 0)),
                  row(d),
                  const((2 * HEAD_WIDTH, d)), const((1, d)), const((d, LANES)), const((1, LANES))],
        out_specs=[row(d), row(d), row(LANES), row(LANES), const((8, LANES))],
        out_shape=[jax.ShapeDtypeStruct((n, d), F32),
                   jax.ShapeDtypeStruct((n, d), F32),
                   jax.ShapeDtypeStruct((n, LANES), I32),
                   jax.ShapeDtypeStruct((n, LANES), F32),
                   jax.ShapeDtypeStruct((8, LANES), I32)],
        scratch_shapes=[pltpu.VMEM((8, LANES), F32)],
        compiler_params=_cparams(("arbitrary",)),
        name="mix",
    )(hg, da_p, da_s, x, wo_bf16, g2, wr_pad, br_pad)


def _dispatch_copy(h_ref, xs_ref, sem, dest_ref, r, kk):
    return pltpu.make_async_copy(h_ref.at[pl.ds(r, 1)],
                                 xs_ref.at[pl.ds(dest_ref[r * TOP_K + kk], 1)], sem)


def _dispatch_kernel(dest_ref, h_ref, xs_in_ref, xs_ref, sem):
    del xs_in_ref
    tm = h_ref.shape[0]

    def issue(r, carry):
        for kk in range(TOP_K):
            _dispatch_copy(h_ref, xs_ref, sem, dest_ref, r, kk).start()
        return carry

    def drain(r, carry):
        for kk in range(TOP_K):
            _dispatch_copy(h_ref, xs_ref, sem, dest_ref, r, kk).wait()
        return carry

    lax.fori_loop(0, tm, issue, 0)
    lax.fori_loop(0, tm, drain, 0)


def _dispatch(h2, dest_flat, xs_init, tm):
    n, d = h2.shape
    return pl.pallas_call(
        _dispatch_kernel,
        grid=(n // tm,),
        in_specs=[pl.BlockSpec((tm * TOP_K,), lambda i: (i,), memory_space=pltpu.SMEM),
                  pl.BlockSpec((tm, d), lambda i: (i, 0)),
                  pl.BlockSpec(memory_space=pl.ANY)],
        out_specs=pl.BlockSpec(memory_space=pl.ANY),
        out_shape=jax.ShapeDtypeStruct(xs_init.shape, xs_init.dtype),
        input_output_aliases={2: 0},
        scratch_shapes=[pltpu.SemaphoreType.DMA],
        compiler_params=pltpu.CompilerParams(dimension_semantics=("arbitrary",),
                                             vmem_limit_bytes=VMEM_LIMIT,
                                             has_side_effects=True),
        name="dispatch",
    )(dest_flat, h2, xs_init)


def _moe_kernel(be_ref, nu_ref, xs_ref, wg_ref, wu_ref, bg_ref, bu_ref, wd_ref, bd_ref,
                o_ref, xb_scr):
    i = pl.program_id(0)
    j = pl.program_id(1)
    used = i < nu_ref[0]

    @pl.when(jnp.logical_and(j == 0, used))
    def _():
        xb_scr[...] = xs_ref[...].astype(BF16)
        o_ref[...] = jnp.broadcast_to(bd_ref[0], o_ref.shape)

    @pl.when(jnp.logical_and(j == 0, jnp.logical_not(used)))
    def _():
        o_ref[...] = jnp.zeros(o_ref.shape, F32)

    @pl.when(used)
    def _():
        xb = xb_scr[...]
        gate = _dot(xb, wg_ref[0].astype(BF16)) + bg_ref[0]
        up = _dot(xb, wu_ref[0].astype(BF16)) + bu_ref[0]
        gate = jnp.minimum(gate, SWIGLU_LIMIT)
        up = jnp.clip(up, -SWIGLU_LIMIT, SWIGLU_LIMIT)
        glu = gate * _sigmoid(SWIGLU_ALPHA * gate)
        mid = ((up + 1.0) * glu).astype(BF16)
        o_ref[...] += _dot(mid, wd_ref[0].astype(BF16))


def _moe(xs, block_expert, n_used, w_gu, b_gu, w_down, b_down, bm, th):
    rows, d = xs.shape
    n_e, _, two_h = w_gu.shape
    hid = two_h // 2
    nj = hid // th
    nb = rows // bm
    b_gu3 = b_gu.reshape(n_e, 1, two_h)
    b_dn3 = b_down.reshape(n_e, 1, d)

    def xrow(i, j, be, nu):
        return (jnp.minimum(i, jnp.maximum(nu[0] - 1, 0)), 0)

    def hcol(i, j, nu):
        return jnp.where(i < nu[0], j, nj - 1)

    grid_spec = pltpu.PrefetchScalarGridSpec(
        num_scalar_prefetch=2,
        grid=(nb, nj),
        in_specs=[pl.BlockSpec((bm, d), xrow),
                  pl.BlockSpec((1, d, th), lambda i, j, be, nu: (be[i], 0, hcol(i, j, nu))),
                  pl.BlockSpec((1, d, th), lambda i, j, be, nu: (be[i], 0, nj + hcol(i, j, nu))),
                  pl.BlockSpec((1, 1, th), lambda i, j, be, nu: (be[i], 0, hcol(i, j, nu))),
                  pl.BlockSpec((1, 1, th), lambda i, j, be, nu: (be[i], 0, nj + hcol(i, j, nu))),
                  pl.BlockSpec((1, th, d), lambda i, j, be, nu: (be[i], hcol(i, j, nu), 0)),
                  pl.BlockSpec((1, 1, d), lambda i, j, be, nu: (be[i], 0, 0))],
        out_specs=pl.BlockSpec((bm, d), lambda i, j, be, nu: (i, 0)),
        scratch_shapes=[pltpu.VMEM((bm, d), BF16)])
    return pl.pallas_call(
        _moe_kernel,
        grid_spec=grid_spec,
        out_shape=jax.ShapeDtypeStruct((rows, d), F32),
        compiler_params=_cparams(("arbitrary", "arbitrary")),
        name="moe",
    )(block_expert, n_used, xs, w_gu, w_gu, b_gu3, b_gu3, w_down, b_dn3)


def _combine_copy(yb_ref, buf, sem, dest_ref, r, kk):
    return pltpu.make_async_copy(yb_ref.at[pl.ds(dest_ref[r * TOP_K + kk], 1)],
                                 buf.at[kk, pl.ds(r, 1)], sem)


def _combine_kernel(n_prompt_tiles, dest_ref, x1_ref, gates_ref, yb_ref, yp_ref, ys_ref,
                    buf, sem):
    i = pl.program_id(0)
    tm = x1_ref.shape[0]

    def issue(r, carry):
        for kk in range(TOP_K):
            _combine_copy(yb_ref, buf, sem, dest_ref, r, kk).start()
        return carry

    def drain(r, carry):
        for kk in range(TOP_K):
            _combine_copy(yb_ref, buf, sem, dest_ref, r, kk).wait()
        return carry

    lax.fori_loop(0, tm, issue, 0)
    lax.fori_loop(0, tm, drain, 0)

    g = gates_ref[...]
    y = x1_ref[...]
    for kk in range(TOP_K):
        y = y + g[:, kk:kk + 1] * buf[kk]

    @pl.when(i < n_prompt_tiles)
    def _():
        yp_ref[...] = y

    @pl.when(i >= n_prompt_tiles)
    def _():
        ys_ref[...] = y


def _combine(x1, gates, dest_flat, yb, n_prompt, tm):
    n, d = x1.shape
    npt = n_prompt // tm
    return pl.pallas_call(
        functools.partial(_combine_kernel, npt),
        grid=(n // tm,),
        in_specs=[pl.BlockSpec((tm * TOP_K,), lambda i: (i,), memory_space=pltpu.SMEM),
                  pl.BlockSpec((tm, d), lambda i: (i, 0)),
                  pl.BlockSpec((tm, LANES), lambda i: (i, 0)),
                  pl.BlockSpec(memory_space=pl.ANY)],
        out_specs=[pl.BlockSpec((tm, d), lambda i: (jnp.minimum(i, npt - 1), 0)),
                   pl.BlockSpec((tm, d), lambda i: (jnp.maximum(i - npt, 0), 0))],
        out_shape=[jax.ShapeDtypeStruct((n_prompt, d), F32),
                   jax.ShapeDtypeStruct((n - n_prompt, d), F32)],
        scratch_shapes=[pltpu.VMEM((TOP_K, tm, d), F32), pltpu.SemaphoreType.DMA],
        compiler_params=_cparams(("arbitrary",)),
        name="combine",
    )(dest_flat, x1, gates, yb)


def _group_matrix(group):
    r = jnp.arange(256) // group
    return (r[:, None] == r[None, :]).astype(BF16)


def _tile(n, want):
    t = min(want, n)
    while n % t:
        t //= 2
    return t


def _layer(x_prompt, x_sample, cache_k, cache_v, state_hgrn, attn_norm_g, w_in, lb, hg_norm_g,
           da_q_norm_g, da_k_norm_g, da_lambda, da_subln_g, w_out, ffn_norm_g, w_router,
           b_router, w_gu, b_gu, w_down, b_down, lam_init):
    bp, s, d = x_prompt.shape
    bs, t, _ = x_sample.shape
    assert bp == 1 and t == HG_TILE and s % HG_TILE == 0
    n_p = bp * s
    n = n_p + bs * t
    x = jnp.concatenate([x_prompt.reshape(n_p, d), x_sample.reshape(bs * t, d)], axis=0)

    proj = _proj(x, attn_norm_g.reshape(1, d), w_in.astype(BF16), _tile(n, PROJ_TM), PROJ_TN)

    s0_all = jnp.concatenate([jnp.zeros((1,) + state_hgrn.shape[1:], F32), state_hgrn], axis=0)
    hg, s_all = _hgrn(proj, lb.reshape(1, HEAD_WIDTH), hg_norm_g.reshape(1, HG_DIM), s0_all,
                      n_p // HG_TILE)

    tm_da = _tile(math.gcd(n_p, bs * t), DA_TM)
    gq = jnp.tile(da_q_norm_g, 2 * DA_HEADS).reshape(1, HEAD_WIDTH)
    gk = jnp.tile(da_k_norm_g, 2 * DA_HEADS).reshape(1, HEAD_WIDTH)
    qlo, qhi, kb, vt, k_p, k_s, v_p, v_s = _da_prep(proj, gq, gk, _group_matrix(DA_QK_DIM),
                                                    n_p, tm_da)
    sg = da_subln_g.reshape(1, DA_V_DIM)
    da_p = _attn_prompt(qlo, qhi, kb, vt, da_lambda, sg, n_p, lam_init, _tile(n_p, ATTN_TILE))
    past = cache_k.shape[1]
    da_s = _attn_sample(qlo, qhi, k_s, v_s, cache_k.reshape(bs, past, HEAD_WIDTH),
                        cache_v.reshape(bs, past, HEAD_WIDTH), da_lambda, sg, n_p, lam_init)

    wr_pad = jnp.pad(w_router, ((0, 0), (0, LANES - N_EXPERTS)))
    br_pad = jnp.pad(b_router, (0, LANES - N_EXPERTS)).reshape(1, LANES)
    x1, h2, route, gates, cnt = _mix(hg, da_p, da_s, x, w_out.astype(BF16),
                                     ffn_norm_g.reshape(1, d), wr_pad, br_pad,
                                     _tile(math.gcd(n_p, bs * t), MIX_TM))

    bm = MOE_BM
    counts = cnt[0, :N_EXPERTS]
    padded = (counts + bm - 1) // bm * bm
    pend = jnp.cumsum(padded)
    pstart = pend - padded
    dest = (pstart[route[:, 0:TOP_K]] + route[:, TOP_K:2 * TOP_K]).reshape(-1).astype(I32)
    nb = (n * TOP_K + N_EXPERTS * (bm - 1)) // bm
    n_used = (pend[-1] // bm).astype(I32)
    blk = jnp.minimum(jnp.arange(nb, dtype=I32), jnp.maximum(n_used - 1, 0))
    block_expert = jnp.minimum(jnp.searchsorted(pend, blk * bm, side='right'),
                               N_EXPERTS - 1).astype(I32)

    xs = _dispatch(h2, dest, jnp.zeros((nb * bm, d), F32), _tile(n, DISPATCH_TM))
    yb = _moe(xs, block_expert, n_used.reshape(1), w_gu, b_gu, w_down, b_down, bm, min(MOE_TH, w_down.shape[1]))
    y_p, y_s = _combine(x1, gates, dest, yb, n_p, _tile(math.gcd(n_p, bs * t), COMBINE_TM))

    return (y_p.reshape(bp, s, d), y_s.reshape(bs, t, d),
            k_p.reshape(bp, s, DA_HEADS, 2, DA_QK_DIM), v_p.reshape(bp, s, DA_HEADS, DA_V_DIM),
            s_all[0:1],
            k_s.reshape(bs, t, DA_HEADS, 2, DA_QK_DIM), v_s.reshape(bs, t, DA_HEADS, DA_V_DIM),
            s_all[1:])


def kernel(x_prompt, x_sample, cache_k, cache_v, state_hgrn, attn_norm_g, w_in, hg_lb_logits,
           hg_norm_g, da_q_norm_g, da_k_norm_g, da_lambda, da_subln_g, w_out, ffn_norm_g,
           w_router, b_router, w_gu, b_gu, w_down, b_down):
    depth = w_in.shape[0]
    assert depth == 1, "single-layer step"
    lb_all = jnp.cumsum(jax.nn.softmax(hg_lb_logits.astype(F32), axis=0), axis=0)
    lam_init = 0.8 - 0.6 * math.exp(-0.3 * 0)
    outs = _layer(x_prompt, x_sample, cache_k[0], cache_v[0], state_hgrn[0], attn_norm_g[0],
                  w_in[0], lb_all[0], hg_norm_g[0], da_q_norm_g[0], da_k_norm_g[0], da_lambda[0],
                  da_subln_g[0], w_out[0], ffn_norm_g[0], w_router[0], b_router[0], w_gu[0],
                  b_gu[0], w_down[0], b_down[0], lam_init)
    y_p, y_s, k_p, v_p, s_p, k_s, v_s, s_s = outs
    return (y_p, y_s, k_p[None], v_p[None], s_p[None], k_s[None], v_s[None], s_s[None])
```

```python
import functools
import math

import jax
import jax.numpy as jnp
from jax import lax
from jax.experimental import pallas as pl
from jax.experimental.pallas import tpu as pltpu

F32 = jnp.float32
BF16 = jnp.bfloat16
I32 = jnp.int32
U32 = jnp.uint32

RMS_EPS = 1e-6
CHUNK = 64
HG_HEADS = 8
HG_DIM = 128
DA_HEADS = 8
DA_QK_DIM = 64
DA_V_DIM = 128
HEAD_WIDTH = 1024
N_EXPERTS = 32
TOP_K = 4
SWIGLU_LIMIT = 7.0
SWIGLU_ALPHA = 1.702
LANES = 128
HG_TILE = 64
HG_SUB = 16
EXP_CLAMP = 80.0
LOG2_E = 1.4426950408889634
V_ROWS = DA_V_DIM + 16
VMEM_LIMIT = 56 * 1024 * 1024
PROJ_TM, PROJ_TN = 512, 1024
DA_TM = 256
ATTN_TILE = 512
MIX_TM = 256
DISPATCH_TM = 256
MOE_BM, MOE_TH = 1024, 256
COMBINE_TM = 128


def _cparams(sem, vmem=VMEM_LIMIT):
    return pltpu.CompilerParams(dimension_semantics=sem, vmem_limit_bytes=vmem)


def _dot(a, b):
    return jnp.dot(a, b, preferred_element_type=F32)


def _dot_nt(a, b):
    return lax.dot_general(a, b, (((1,), (1,)), ((), ())), preferred_element_type=F32)


def _split_bf16(x):
    hi = x.astype(BF16)
    lo = (x - hi.astype(F32)).astype(BF16)
    return hi, lo


def _sigmoid(x):
    return 1.0 / (1.0 + jnp.exp(-x))


def _group_sum(sq, gmat):
    outs = []
    for c in range(sq.shape[1] // 256):
        hi, lo = _split_bf16(sq[:, c * 256:(c + 1) * 256])
        outs.append(_dot(hi, gmat) + _dot(lo, gmat))
    return jnp.concatenate(outs, axis=1)


def _proj_kernel(n_prompt_tiles, xp_ref, xs_ref, g_ref, w_ref, o_ref, h_scr):
    i = pl.program_id(0)
    first_col = pl.program_id(1) == 0

    def normalise(x_ref):
        x = x_ref[...]
        ms = jnp.mean(x * x, axis=-1, keepdims=True)
        h_scr[...] = ((x * lax.rsqrt(ms + RMS_EPS)) * g_ref[...]).astype(BF16)

    @pl.when(jnp.logical_and(first_col, i < n_prompt_tiles))
    def _():
        normalise(xp_ref)

    @pl.when(jnp.logical_and(first_col, i >= n_prompt_tiles))
    def _():
        normalise(xs_ref)

    o_ref[...] = _dot(h_scr[...], w_ref[...])


def _proj(x_p, x_s, g, w_bf16, tm, tn):
    d = x_p.shape[1]
    n = x_p.shape[0] + x_s.shape[0]
    npt = x_p.shape[0] // tm
    cols = w_bf16.shape[1]
    return pl.pallas_call(
        functools.partial(_proj_kernel, npt),
        grid=(n // tm, cols // tn),
        in_specs=[pl.BlockSpec((tm, d), lambda i, j: (jnp.minimum(i, npt - 1), 0)),
                  pl.BlockSpec((tm, d), lambda i, j: (jnp.maximum(i - npt, 0), 0)),
                  pl.BlockSpec((1, d), lambda i, j: (0, 0)),
                  pl.BlockSpec((d, tn), lambda i, j: (0, j))],
        out_specs=pl.BlockSpec((tm, tn), lambda i, j: (i, j)),
        out_shape=jax.ShapeDtypeStruct((n, cols), F32),
        scratch_shapes=[pltpu.VMEM((tm, d), BF16)],
        compiler_params=_cparams(("arbitrary", "arbitrary")),
        name="proj",
    )(x_p, x_s, g, w_bf16)


def _hgrn_kernel(n_prompt_tiles, q_ref, f_ref, i_ref, g_ref, lb_ref, ng_ref, s0_ref,
                 o_ref, sout_ref, st_scr):
    t = pl.program_id(0)
    starts = jnp.logical_or(t == 0, t >= n_prompt_tiles)
    ends = t >= n_prompt_tiles - 1

    @pl.when(starts)
    def _():
        for h in range(HG_HEADS):
            st_scr[h] = s0_ref[0, h].T

    lb = lb_ref[...]
    q = q_ref[...]
    q = q * _sigmoid(q)
    f = lb + (1.0 - lb) * _sigmoid(f_ref[...])
    logf = jnp.log(f)
    k = 1.0 - f
    v = i_ref[...]
    gate = _sigmoid(g_ref[...])

    row = lax.broadcasted_iota(I32, (HG_TILE, HG_TILE), 0)
    col = lax.broadcasted_iota(I32, (HG_TILE, HG_TILE), 1)
    causal = col <= row
    tri = causal.astype(BF16)
    hi, lo = _split_bf16(logf)
    b = _dot(tri, hi) + _dot(tri, lo)
    b_last = b[HG_TILE - 1:HG_TILE, :]
    q_dec = q * jnp.exp(b)
    k_dec = k * jnp.exp(b_last - b)
    ng = ng_ref[...]

    outs = []
    for h in range(HG_HEADS):
        sl = slice(h * HG_DIM, (h + 1) * HG_DIM)
        st = st_scr[h]
        bh = b[:, sl]
        qh = q[:, sl]
        kh = k[:, sl]
        vh = v[:, sl].astype(BF16)
        o = _dot_nt(q_dec[:, sl].astype(BF16), st.astype(BF16))
        rows = []
        for i in range(HG_TILE // HG_SUB):
            r0 = i * HG_SUB
            base = bh[r0 - 1:r0, :] if i else jnp.zeros((1, HG_DIM), F32)
            qi = qh[r0:r0 + HG_SUB, :] * jnp.exp(bh[r0:r0 + HG_SUB, :] - base)
            ki = kh * jnp.exp(jnp.minimum(base - bh, EXP_CLAMP))
            rows.append(_dot_nt(qi.astype(BF16), ki.astype(BF16)))
        scores = jnp.where(causal, jnp.concatenate(rows, axis=0), 0.0)
        o = o + _dot(scores.astype(BF16), vh)
        st_new = st * jnp.exp(b_last[:, sl]) + _dot(v[:, sl].T.astype(BF16),
                                                    k_dec[:, sl].astype(BF16))
        st_scr[h] = st_new
        ms = jnp.mean(o * o, axis=-1, keepdims=True)
        outs.append((o * lax.rsqrt(ms + RMS_EPS)) * ng)
    o_ref[...] = (jnp.concatenate(outs, axis=1) * gate).astype(o_ref.dtype)

    @pl.when(ends)
    def _():
        for h in range(HG_HEADS):
            sout_ref[0, h] = st_scr[h].T


def _hgrn(proj, lb, norm_g, s0_all, n_prompt_tiles):
    n = proj.shape[0]
    n_tiles = n // HG_TILE
    seq = lambda t: (jnp.maximum(t - (n_prompt_tiles - 1), 0), 0, 0, 0)
    col_spec = lambda c: pl.BlockSpec((HG_TILE, HEAD_WIDTH), lambda t, c=c: (t, c))
    return pl.pallas_call(
        functools.partial(_hgrn_kernel, n_prompt_tiles),
        grid=(n_tiles,),
        in_specs=[col_spec(0), col_spec(1), col_spec(2), col_spec(3),
                  pl.BlockSpec((1, HEAD_WIDTH), lambda t: (0, 0)),
                  pl.BlockSpec((1, HG_DIM), lambda t: (0, 0)),
                  pl.BlockSpec((1, HG_HEADS, HG_DIM, HG_DIM), seq)],
        out_specs=[pl.BlockSpec((HG_TILE, HEAD_WIDTH), lambda t: (t, 0)),
                   pl.BlockSpec((1, HG_HEADS, HG_DIM, HG_DIM), seq)],
        out_shape=[jax.ShapeDtypeStruct((n, HEAD_WIDTH), BF16),
                   jax.ShapeDtypeStruct(s0_all.shape, F32)],
        scratch_shapes=[pltpu.VMEM((HG_HEADS, HG_DIM, HG_DIM), F32)],
        compiler_params=_cparams(("arbitrary",)),
        name="hgrn",
    )(proj, proj, proj, proj, lb, norm_g, s0_all)


def _da_prep_kernel(n_prompt_tiles, dq_ref, dk_ref, dv_ref, gq_ref, gk_ref, gmat_ref,
                    qlo_ref, qhi_ref, kb_ref, vt_ref, kp_ref, ks_ref, vp_ref, vs_ref):
    i = pl.program_id(0)
    gmat = gmat_ref[...]
    inv = 1.0 / DA_QK_DIM

    dq = dq_ref[...]
    qn = dq * lax.rsqrt(_group_sum(dq * dq, gmat) * inv + RMS_EPS) * gq_ref[...]
    qn = qn * (DA_QK_DIM ** -0.5 * LOG2_E)
    lane = lax.broadcasted_iota(I32, qn.shape, 1)
    first = (lane & DA_QK_DIM) == 0
    qlo_ref[...] = jnp.where(first, qn, 0.0).astype(BF16)
    qhi_ref[...] = jnp.where(first, 0.0, qn).astype(BF16)

    dk = dk_ref[...]
    kn = dk * lax.rsqrt(_group_sum(dk * dk, gmat) * inv + RMS_EPS) * gk_ref[...]
    kb_ref[...] = kn.astype(BF16)
    v = dv_ref[...]
    vt = v.T
    tail = (lax.broadcasted_iota(I32, (V_ROWS - DA_V_DIM, vt.shape[1]), 0) == 0).astype(F32)
    pieces = []
    for h in range(DA_HEADS):
        pieces += [vt[h * DA_V_DIM:(h + 1) * DA_V_DIM, :], tail]
    vt_ref[...] = jnp.concatenate(pieces, axis=0).astype(BF16)

    @pl.when(i < n_prompt_tiles)
    def _():
        kp_ref[...] = kn
        vp_ref[...] = v

    @pl.when(i >= n_prompt_tiles)
    def _():
        ks_ref[...] = kn
        vs_ref[...] = v


def _da_prep(proj, gq, gk, gmat, n_prompt, tm):
    n = proj.shape[0]
    npt = n_prompt // tm
    full = lambda: pl.BlockSpec((tm, HEAD_WIDTH), lambda i: (i, 0))
    p_spec = lambda: pl.BlockSpec((tm, HEAD_WIDTH), lambda i: (jnp.minimum(i, npt - 1), 0))
    s_spec = lambda: pl.BlockSpec((tm, HEAD_WIDTH), lambda i: (jnp.maximum(i - npt, 0), 0))
    vec = lambda: pl.BlockSpec((1, HEAD_WIDTH), lambda i: (0, 0))
    return pl.pallas_call(
        functools.partial(_da_prep_kernel, npt),
        grid=(n // tm,),
        in_specs=[pl.BlockSpec((tm, HEAD_WIDTH), lambda i: (i, 4)),
                  pl.BlockSpec((tm, HEAD_WIDTH), lambda i: (i, 5)),
                  pl.BlockSpec((tm, HEAD_WIDTH), lambda i: (i, 6)),
                  vec(), vec(),
                  pl.BlockSpec((256, 256), lambda i: (0, 0))],
        out_specs=[full(), full(), full(),
                   pl.BlockSpec((DA_HEADS * V_ROWS, tm), lambda i: (0, i)),
                   p_spec(), s_spec(), p_spec(), s_spec()],
        out_shape=[jax.ShapeDtypeStruct((n, HEAD_WIDTH), BF16),
                   jax.ShapeDtypeStruct((n, HEAD_WIDTH), BF16),
                   jax.ShapeDtypeStruct((n, HEAD_WIDTH), BF16),
                   jax.ShapeDtypeStruct((DA_HEADS * V_ROWS, n), BF16),
                   jax.ShapeDtypeStruct((n_prompt, HEAD_WIDTH), F32),
                   jax.ShapeDtypeStruct((n - n_prompt, HEAD_WIDTH), F32),
                   jax.ShapeDtypeStruct((n_prompt, HEAD_WIDTH), F32),
                   jax.ShapeDtypeStruct((n - n_prompt, HEAD_WIDTH), F32)],
        compiler_params=_cparams(("arbitrary",)),
        name="da_prep",
    )(proj, proj, proj, gq, gk, gmat)


def _lambda_of(lp, lam_init):
    a = jnp.sum(lp[0:1, :] * lp[1:2, :], axis=-1, keepdims=True)
    b = jnp.sum(lp[2:3, :] * lp[3:4, :], axis=-1, keepdims=True)
    return jnp.exp(a) - jnp.exp(b) + lam_init


def _subln(o, g, lam_init):
    ms = jnp.mean(o * o, axis=-1, keepdims=True)
    return (o * lax.rsqrt(ms + RMS_EPS)) * g * (1.0 - lam_init)


def _attn_p_kernel(lam_init, tile, qi_ref, kj_ref, qlo_ref, qhi_ref, k_ref, vt_ref, lp_ref,
                   sg_ref, o_ref, m_scr, acc_scr):
    s = pl.program_id(0)
    qi = qi_ref[s]
    kj = kj_ref[s]
    n_maps = 2 * DA_HEADS

    @pl.when(kj == 0)
    def _():
        m_scr[...] = jnp.full(m_scr.shape, -jnp.inf, F32)
        acc_scr[...] = jnp.zeros(acc_scr.shape, F32)

    def scores(r, visible):
        h, c = divmod(r, 2)
        sl = slice(h * DA_V_DIM, (h + 1) * DA_V_DIM)
        sc = _dot_nt(k_ref[:, sl], (qlo_ref, qhi_ref)[c][:, sl])
        return sc if visible is None else jnp.where(visible, sc, -jnp.inf)

    def sweep(visible):
        sc = scores(0, visible)
        for r in range(n_maps):
            nxt = scores(r + 1, visible) if r + 1 < n_maps else None
            h = r // 2
            m_prev = m_scr[r:r + 1, :]
            m_new = jnp.maximum(m_prev, jnp.max(sc, axis=0, keepdims=True))
            alpha = jnp.exp2(m_prev - m_new)
            e = jnp.exp2(sc - m_new).astype(BF16)
            acc_scr[r] = alpha * acc_scr[r] + _dot(vt_ref[h * V_ROWS:(h + 1) * V_ROWS, :], e)
            m_scr[r:r + 1, :] = m_new
            sc = nxt

    @pl.when(kj < qi)
    def _():
        sweep(None)

    @pl.when(kj == qi)
    def _():
        shift = CHUNK.bit_length() - 1
        key_chunk = lax.broadcasted_iota(I32, (tile, tile), 0) >> shift
        qry_chunk = lax.broadcasted_iota(I32, (tile, tile), 1) >> shift
        sweep(key_chunk <= qry_chunk)
        lam = _lambda_of(lp_ref[...], lam_init)
        sg = sg_ref[...]
        outs = []
        for h in range(DA_HEADS):
            acc0 = acc_scr[2 * h]
            acc1 = acc_scr[2 * h + 1]
            a0 = acc0[0:DA_V_DIM, :] / acc0[DA_V_DIM:DA_V_DIM + 1, :]
            a1 = acc1[0:DA_V_DIM, :] / acc1[DA_V_DIM:DA_V_DIM + 1, :]
            outs.append(_subln((a0 - lam * a1).T, sg, lam_init))
        o_ref[...] = jnp.concatenate(outs, axis=1).astype(o_ref.dtype)


def _attn_prompt(qlo, qhi, kb, vt, lp, sg, n_prompt, lam_init, tile):
    nq = n_prompt // tile
    qi_l, kj_l = [], []
    for i in range(nq):
        for j in range(i + 1):
            qi_l.append(i)
            kj_l.append(j)
    qi = jnp.asarray(qi_l, I32)
    kj = jnp.asarray(kj_l, I32)
    grid_spec = pltpu.PrefetchScalarGridSpec(
        num_scalar_prefetch=2,
        grid=(len(qi_l),),
        in_specs=[pl.BlockSpec((tile, HEAD_WIDTH), lambda s, qi, kj: (qi[s], 0)),
                  pl.BlockSpec((tile, HEAD_WIDTH), lambda s, qi, kj: (qi[s], 0)),
                  pl.BlockSpec((tile, HEAD_WIDTH), lambda s, qi, kj: (kj[s], 0)),
                  pl.BlockSpec((DA_HEADS * V_ROWS, tile), lambda s, qi, kj: (0, kj[s])),
                  pl.BlockSpec((4, DA_QK_DIM), lambda s, qi, kj: (0, 0)),
                  pl.BlockSpec((1, DA_V_DIM), lambda s, qi, kj: (0, 0))],
        out_specs=pl.BlockSpec((tile, HEAD_WIDTH), lambda s, qi, kj: (qi[s], 0)),
        scratch_shapes=[pltpu.VMEM((2 * DA_HEADS, tile), F32),
                        pltpu.VMEM((2 * DA_HEADS, V_ROWS, tile), F32)])
    return pl.pallas_call(
        functools.partial(_attn_p_kernel, lam_init, tile),
        grid_spec=grid_spec,
        out_shape=jax.ShapeDtypeStruct((n_prompt, HEAD_WIDTH), BF16),
        compiler_params=_cparams(("arbitrary",)),
        name="attn_p",
    )(qi, kj, qlo, qhi, kb, vt, lp, sg)


def _attn_s_kernel(lam_init, qlo_ref, qhi_ref, kn_ref, vn_ref, ck_ref, cv_ref, lp_ref, sg_ref,
                   o_ref):
    lam =_lambda_of(lp_ref[...], lam_init)
    sg = sg_ref[...]
    outs = []
    for h in range(DA_HEADS):
        sl = slice(h * DA_V_DIM, (h + 1) * DA_V_DIM)
        kc = ck_ref[0, :, sl].astype(BF16)
        kn = kn_ref[:, sl].astype(BF16)
        vc = cv_ref[0, :, sl].astype(BF16)
        vn = vn_ref[:, sl].astype(BF16)
        acc = None
        for c, q_ref in enumerate((qlo_ref, qhi_ref)):
            qh = q_ref[:, sl]
            s_c = _dot_nt(qh, kc)
            s_n = _dot_nt(qh, kn)
            m = jnp.maximum(jnp.max(s_c, axis=-1, keepdims=True),
                            jnp.max(s_n, axis=-1, keepdims=True))
            e_c = jnp.exp2(s_c - m)
            e_n = jnp.exp2(s_n - m)
            l = jnp.sum(e_c, axis=-1, keepdims=True) + jnp.sum(e_n, axis=-1, keepdims=True)
            part = (_dot(e_c.astype(BF16), vc) + _dot(e_n.astype(BF16), vn)) / l
            acc = part if c == 0 else acc - lam * part
        outs.append(_subln(acc, sg, lam_init))
    o_ref[...] = jnp.concatenate(outs, axis=1).astype(o_ref.dtype)


def _attn_sample(qlo, qhi, ks, vs, cache_k, cache_v, lp, sg, n_prompt, lam_init):
    bs, past = cache_k.shape[0], cache_k.shape[1]
    t = ks.shape[0] // bs
    off = n_prompt // t
    return pl.pallas_call(
        functools.partial(_attn_s_kernel, lam_init),
        grid=(bs,),
        in_specs=[pl.BlockSpec((t, HEAD_WIDTH), lambda b: (off + b, 0)),
                  pl.BlockSpec((t, HEAD_WIDTH), lambda b: (off + b, 0)),
                  pl.BlockSpec((t, HEAD_WIDTH), lambda b: (b, 0)),
                  pl.BlockSpec((t, HEAD_WIDTH), lambda b: (b, 0)),
                  pl.BlockSpec((1, past, HEAD_WIDTH), lambda b: (b, 0, 0)),
                  pl.BlockSpec((1, past, HEAD_WIDTH), lambda b: (b, 0, 0)),
                  pl.BlockSpec((4, DA_QK_DIM), lambda b: (0, 0)),
                  pl.BlockSpec((1, DA_V_DIM), lambda b: (0, 0))],
        out_specs=pl.BlockSpec((t, HEAD_WIDTH), lambda b: (b, 0)),
        out_shape=jax.ShapeDtypeStruct((bs * t, HEAD_WIDTH), BF16),
        compiler_params=_cparams(("arbitrary",)),
        name="attn_s",
    )(qlo, qhi, ks, vs, cache_k, cache_v, lp, sg)


def _pack_bf16_pairs(x):
    half = x.shape[1] // 2
    lo = lax.bitcast_convert_type(x[:, :half].astype(BF16).astype(F32), U32)
    hi = lax.bitcast_convert_type(x[:, half:].astype(BF16).astype(F32), U32)
    return (lo >> 16) | (hi & jnp.uint32(0xFFFF0000))


def _unpack_bf16_pairs(w):
    lo = lax.bitcast_convert_type(w << 16, F32).astype(BF16)
    hi = lax.bitcast_convert_type(w & jnp.uint32(0xFFFF0000), F32).astype(BF16)
    return lo, hi


def _mix_kernel(n_prompt_tiles, hg_ref, dap_ref, das_ref, xp_ref, xs_ref, wo_ref, g2_ref, wr_ref,
                br_ref, x1_ref, h2p_ref, route_ref, gates_ref, cnt_ref, base_scr):
    i = pl.program_id(0)

    @pl.when(i == 0)
    def _():
        base_scr[...] = jnp.zeros(base_scr.shape, F32)

    tm = xp_ref.shape[0]
    is_prompt = i < n_prompt_tiles
    da = jnp.where(is_prompt, dap_ref[...], das_ref[...])
    x = jnp.where(is_prompt, xp_ref[...], xs_ref[...])
    x1 = x + _dot(hg_ref[...], wo_ref[0:HEAD_WIDTH, :]) \
        + _dot(da, wo_ref[HEAD_WIDTH:2 * HEAD_WIDTH, :])
    x1_ref[...] = x1
    ms = jnp.mean(x1 * x1, axis=-1, keepdims=True)
    h2 = (x1 * lax.rsqrt(ms + RMS_EPS)) * g2_ref[...]
    h2p_ref[...] = _pack_bf16_pairs(h2)

    h_hi, h_lo = _split_bf16(h2)
    w_hi, w_lo = _split_bf16(wr_ref[...])
    logits = _dot(h_hi, w_hi) + _dot(h_lo, w_hi) + _dot(h_hi, w_lo) + br_ref[...]
    lane = lax.broadcasted_iota(I32, logits.shape, 1)
    lanef = lane.astype(F32)
    lg = jnp.where(lane < N_EXPERTS, logits, -jnp.inf)

    sels, vals, idxs = [], [], []
    for _ in range(TOP_K):
        mx = jnp.max(lg, axis=-1, keepdims=True)
        idx = jnp.min(jnp.where(lg == mx, lanef, float(LANES)), axis=-1, keepdims=True)
        sel = lanef == idx
        lg = jnp.where(sel, -jnp.inf, lg)
        sels.append(sel)
        vals.append(mx)
        idxs.append(idx)
    es = [jnp.exp(v - vals[0]) for v in vals]
    den = es[0] + es[1] + es[2] + es[3]

    onehot = jnp.zeros(logits.shape, F32)
    for sel in sels:
        onehot = onehot + sel.astype(F32)
    row = lax.broadcasted_iota(I32, (tm, tm), 0)
    col = lax.broadcasted_iota(I32, (tm, tm), 1)
    strict = (col < row).astype(BF16)
    before = _dot(strict, onehot.astype(BF16)) + base_scr[0:1, :]
    base_scr[0:1, :] = base_scr[0:1, :] + jnp.sum(onehot, axis=0, keepdims=True)

    route = jnp.zeros(logits.shape, F32)
    gates = jnp.zeros(logits.shape, F32)
    for r in range(TOP_K):
        rank = jnp.sum(jnp.where(sels[r], before, 0.0), axis=-1, keepdims=True)
        route = jnp.where(lane == r, idxs[r], route)
        route = jnp.where(lane == TOP_K + r, rank, route)
        gates = jnp.where(lane == r, es[r] / den, gates)
    route_ref[...] = route.astype(I32)
    gates_ref[...] = gates
    cnt_ref[...] = jnp.broadcast_to(base_scr[0:1, :], cnt_ref.shape).astype(I32)


def _mix(hg, da_p, da_s, x_p, x_s, wo_bf16, g2, wr_pad, br_pad, tm):
    d = x_p.shape[1]
    n = x_p.shape[0] + x_s.shape[0]
    npt = x_p.shape[0] // tm
    row = lambda w: pl.BlockSpec((tm, w), lambda i: (i, 0))
    p_row = lambda w: pl.BlockSpec((tm, w), lambda i: (jnp.minimum(i, npt - 1), 0))
    s_row = lambda w: pl.BlockSpec((tm, w), lambda i: (jnp.maximum(i - npt, 0), 0))
    const = lambda shp: pl.BlockSpec(shp, lambda i: (0, 0))
    return pl.pallas_call(
        functools.partial(_mix_kernel, npt),
        grid=(n // tm,),
        in_specs=[row(HEAD_WIDTH), p_row(HEAD_WIDTH), s_row(HEAD_WIDTH), p_row(d), s_row(d),
                  const((2 * HEAD_WIDTH, d)), const((1, d)), const((d, LANES)), const((1, LANES))],
        out_specs=[row(d), row(d // 2), row(LANES), row(LANES), const((8, LANES))],
        out_shape=[jax.ShapeDtypeStruct((n, d), F32),
                   jax.ShapeDtypeStruct((n, d // 2), U32),
                   jax.ShapeDtypeStruct((n, LANES), I32),
                   jax.ShapeDtypeStruct((n, LANES), F32),
                   jax.ShapeDtypeStruct((8, LANES), I32)],
        scratch_shapes=[pltpu.VMEM((8, LANES), F32)],
        compiler_params=_cparams(("arbitrary",)),
        name="mix",
    )(hg, da_p, da_s, x_p, x_s, wo_bf16, g2, wr_pad, br_pad)


def _dispatch_copy(h_ref, xs_ref, sem, dest_ref, r, kk):
    return pltpu.make_async_copy(h_ref.at[pl.ds(r, 1)],
                                 xs_ref.at[pl.ds(dest_ref[r * TOP_K + kk], 1)], sem)


def _dispatch_kernel(zinfo_ref, dest_ref, h_ref, xs_ref, zbuf, sem, zsem):
    tm = h_ref.shape[0]
    bm = zbuf.shape[0]
    nb = xs_ref.shape[0] // bm

    def zero_copy(start):
        return pltpu.make_async_copy(zbuf, xs_ref.at[pl.ds(pl.multiple_of(start, bm), bm)], zsem)

    def for_each_zero_block(fn):
        for e in range(N_EXPERTS):
            start = zinfo_ref[e]

            @pl.when(start >= 0)
            def _():
                fn(zero_copy(start))

        def tail(b, carry):
            fn(zero_copy(b * bm))
            return carry

        lax.fori_loop(zinfo_ref[N_EXPERTS], nb, tail, 0)

    @pl.when(pl.program_id(0) == 0)
    def _():
        zbuf[...] = jnp.zeros(zbuf.shape, zbuf.dtype)
        for_each_zero_block(lambda cp: cp.start())
        for_each_zero_block(lambda cp: cp.wait())

    def issue(r, carry):
        for kk in range(TOP_K):
            _dispatch_copy(h_ref, xs_ref, sem, dest_ref, r, kk).start()
        return carry

    def drain(r, carry):
        for kk in range(TOP_K):
            _dispatch_copy(h_ref, xs_ref, sem, dest_ref, r, kk).wait()
        return carry

    lax.fori_loop(0, tm, issue, 0)
    lax.fori_loop(0, tm, drain, 0)


def _dispatch(h2p, dest_flat, zinfo, n_rows, bm, tm):
    n, w = h2p.shape
    grid_spec = pltpu.PrefetchScalarGridSpec(
        num_scalar_prefetch=1,
        grid=(n // tm,),
        in_specs=[pl.BlockSpec((tm * TOP_K,), lambda i, z: (i,), memory_space=pltpu.SMEM),
                  pl.BlockSpec((tm, w), lambda i, z: (i, 0))],
        out_specs=pl.BlockSpec(memory_space=pl.ANY),
        scratch_shapes=[pltpu.VMEM((bm, w), h2p.dtype),
                        pltpu.SemaphoreType.DMA, pltpu.SemaphoreType.DMA])
    return pl.pallas_call(
        _dispatch_kernel,
        grid_spec=grid_spec,
        out_shape=jax.ShapeDtypeStruct((n_rows, w), h2p.dtype),
        compiler_params=_cparams(("arbitrary",)),
        name="dispatch",
    )(zinfo, dest_flat, h2p)


def _moe_kernel(be_ref, nu_ref, xs_ref, wg_ref, wu_ref, bg_ref, bu_ref, wd_ref, bd_ref,
                o_ref, xb_scr):
    i = pl.program_id(0)
    j = pl.program_id(1)
    used = i < nu_ref[0]

    @pl.when(jnp.logical_and(j == 0, used))
    def _():
        half = xs_ref.shape[1]
        lo, hi = _unpack_bf16_pairs(xs_ref[...])
        xb_scr[:, 0:half] = lo
        xb_scr[:, half:2 * half] = hi
        o_ref[...] = jnp.broadcast_to(bd_ref[0], o_ref.shape)

    @pl.when(jnp.logical_and(j == 0, jnp.logical_not(used)))
    def _():
        o_ref[...] = jnp.zeros(o_ref.shape, F32)

    @pl.when(used)
    def _():
        xb = xb_scr[...]
        gate = _dot(xb, wg_ref[0].astype(BF16)) + bg_ref[0]
        up = _dot(xb, wu_ref[0].astype(BF16)) + bu_ref[0]
        gate = jnp.minimum(gate, SWIGLU_LIMIT)
        up = jnp.clip(up, -SWIGLU_LIMIT, SWIGLU_LIMIT)
        glu = gate * _sigmoid(SWIGLU_ALPHA * gate)
        mid = ((up + 1.0) * glu).astype(BF16)
        o_ref[...] += _dot(mid, wd_ref[0].astype(BF16))


def _moe(xs, block_expert, n_used, w_gu, b_gu, w_down, b_down, bm, th):
    rows, half = xs.shape
    d = 2 * half
    n_e, _, two_h = w_gu.shape
    hid = two_h // 2
    nj = hid // th
    nb = rows // bm
    b_gu3 = b_gu.reshape(n_e, 1, two_h)
    b_dn3 = b_down.reshape(n_e, 1, d)

    def xrow(i, j, be, nu):
        return (jnp.minimum(i, jnp.maximum(nu[0] - 1, 0)), 0)

    def hcol(i, j, nu):
        return jnp.where(i < nu[0], j, nj - 1)

    grid_spec = pltpu.PrefetchScalarGridSpec(
        num_scalar_prefetch=2,
        grid=(nb, nj),
        in_specs=[pl.BlockSpec((bm, half), xrow),
                  pl.BlockSpec((1, d, th), lambda i, j, be, nu: (be[i], 0, hcol(i, j, nu))),
                  pl.BlockSpec((1, d, th), lambda i, j, be, nu: (be[i], 0, nj + hcol(i, j, nu))),
                  pl.BlockSpec((1, 1, th), lambda i, j, be, nu: (be[i], 0, hcol(i, j, nu))),
                  pl.BlockSpec((1, 1, th), lambda i, j, be, nu: (be[i], 0, nj + hcol(i, j, nu))),
                  pl.BlockSpec((1, th, d), lambda i, j, be, nu: (be[i], hcol(i, j, nu), 0)),
                  pl.BlockSpec((1, 1, d), lambda i, j, be, nu: (be[i], 0, 0))],
        out_specs=pl.BlockSpec((bm, d), lambda i, j, be, nu: (i, 0)),
        scratch_shapes=[pltpu.VMEM((bm, d), BF16)])
    return pl.pallas_call(
        _moe_kernel,
        grid_spec=grid_spec,
        out_shape=jax.ShapeDtypeStruct((rows, d), F32),
        compiler_params=_cparams(("arbitrary", "arbitrary")),
        name="moe",
    )(block_expert, n_used, xs, w_gu, w_gu, b_gu3, b_gu3, w_down, b_dn3)


def _combine_copy(yb_ref, buf, sem, dest_ref, r, kk):
    return pltpu.make_async_copy(yb_ref.at[pl.ds(dest_ref[r * TOP_K + kk], 1)],
                                 buf.at[kk, pl.ds(r, 1)], sem)


def _combine_kernel(n_prompt_tiles, dest_ref, x1_ref, gates_ref, yb_ref, yp_ref, ys_ref,
                    buf, sem):
    i = pl.program_id(0)
    tm = x1_ref.shape[0]

    def issue(r, carry):
        for kk in range(TOP_K):
            _combine_copy(yb_ref, buf, sem, dest_ref, r, kk).start()
        return carry

    def drain(r, carry):
        for kk in range(TOP_K):
            _combine_copy(yb_ref, buf, sem, dest_ref, r, kk).wait()
        return carry

    lax.fori_loop(0, tm, issue, 0)
    lax.fori_loop(0, tm, drain, 0)

    g = gates_ref[...]
    y = x1_ref[...]
    for kk in range(TOP_K):
        y = y + g[:, kk:kk + 1] * buf[kk]

    @pl.when(i < n_prompt_tiles)
    def _():
        yp_ref[...] = y

    @pl.when(i >= n_prompt_tiles)
    def _():
        ys_ref[...] = y


def _combine(x1, gates, dest_flat, yb, n_prompt, tm):
    n, d = x1.shape
    npt = n_prompt // tm
    return pl.pallas_call(
        functools.partial(_combine_kernel, npt),
        grid=(n // tm,),
        in_specs=[pl.BlockSpec((tm * TOP_K,), lambda i: (i,), memory_space=pltpu.SMEM),
                  pl.BlockSpec((tm, d), lambda i: (i, 0)),
                  pl.BlockSpec((tm, LANES), lambda i: (i, 0)),
                  pl.BlockSpec(memory_space=pl.ANY)],
        out_specs=[pl.BlockSpec((tm, d), lambda i: (jnp.minimum(i, npt - 1), 0)),
                   pl.BlockSpec((tm, d), lambda i: (jnp.maximum(i - npt, 0), 0))],
        out_shape=[jax.ShapeDtypeStruct((n_prompt, d), F32),
                   jax.ShapeDtypeStruct((n - n_prompt, d), F32)],
        scratch_shapes=[pltpu.VMEM((TOP_K, tm, d), F32), pltpu.SemaphoreType.DMA],
        compiler_params=_cparams(("arbitrary",)),
        name="combine",
    )(dest_flat, x1, gates, yb)


def _group_matrix(group):
    r = jnp.arange(256) // group
    return (r[:, None] == r[None, :]).astype(BF16)


def _tile(n, want):
    t = min(want, n)
    while n % t:
        t //= 2
    return t


def _layer(x_prompt, x_sample, cache_k, cache_v, state_hgrn, attn_norm_g, w_in, lb, hg_norm_g,
           da_q_norm_g, da_k_norm_g, da_lambda, da_subln_g, w_out, ffn_norm_g, w_router,
           b_router, w_gu, b_gu, w_down, b_down, lam_init):
    bp, s, d = x_prompt.shape
    bs, t, _ = x_sample.shape
    assert bp == 1 and t == HG_TILE and s % HG_TILE == 0
    n_p = bp * s
    n = n_p + bs * t
    x_p = x_prompt.reshape(n_p, d)
    x_s = x_sample.reshape(bs * t, d)
    row_gcd = math.gcd(n_p, bs * t)

    proj = _proj(x_p, x_s, attn_norm_g.reshape(1, d), w_in.astype(BF16), _tile(row_gcd, PROJ_TM),
                 PROJ_TN)

    s0_all = jnp.concatenate([jnp.zeros((1,) + state_hgrn.shape[1:], F32), state_hgrn], axis=0)
    hg, s_all = _hgrn(proj, lb.reshape(1, HEAD_WIDTH), hg_norm_g.reshape(1, HG_DIM), s0_all,
                      n_p // HG_TILE)

    tm_da = _tile(row_gcd, DA_TM)
    gq = jnp.tile(da_q_norm_g, 2 * DA_HEADS).reshape(1, HEAD_WIDTH)
    gk = jnp.tile(da_k_norm_g, 2 * DA_HEADS).reshape(1, HEAD_WIDTH)
    qlo, qhi, kb, vt, k_p, k_s, v_p, v_s = _da_prep(proj, gq, gk, _group_matrix(DA_QK_DIM),
                                                    n_p, tm_da)
    sg = da_subln_g.reshape(1, DA_V_DIM)
    da_p = _attn_prompt(qlo, qhi, kb, vt, da_lambda, sg, n_p, lam_init, _tile(n_p, ATTN_TILE))
    past = cache_k.shape[1]
    da_s = _attn_sample(qlo, qhi, k_s, v_s, cache_k.reshape(bs, past, HEAD_WIDTH),
                        cache_v.reshape(bs, past, HEAD_WIDTH), da_lambda, sg, n_p, lam_init)

    wr_pad = jnp.pad(w_router, ((0, 0), (0, LANES - N_EXPERTS)))
    br_pad = jnp.pad(b_router, (0, LANES - N_EXPERTS)).reshape(1, LANES)
    x1, h2p, route, gates, cnt = _mix(hg, da_p, da_s, x_p, x_s, w_out.astype(BF16),
                                      ffn_norm_g.reshape(1, d), wr_pad, br_pad,
                                      _tile(row_gcd, MIX_TM))

    bm = MOE_BM
    counts = cnt[0, :N_EXPERTS]
    padded = (counts + bm - 1) // bm * bm
    pend = jnp.cumsum(padded)
    pstart = pend - padded
    dest = (pstart[route[:, 0:TOP_K]] + route[:, TOP_K:2 * TOP_K]).reshape(-1).astype(I32)
    nb = (n * TOP_K + N_EXPERTS * (bm - 1)) // bm
    n_used = (pend[-1] // bm).astype(I32)
    blk = jnp.minimum(jnp.arange(nb, dtype=I32), jnp.maximum(n_used - 1, 0))
    block_expert = jnp.minimum(jnp.sum(pend[None, :] <= (blk * bm)[:, None], axis=1),
                               N_EXPERTS - 1).astype(I32)
    zinfo = jnp.concatenate([jnp.where(padded > 0, pend - bm, -1), n_used[None]]).astype(I32)

    xs = _dispatch(h2p, dest, zinfo, nb * bm, bm, _tile(n, DISPATCH_TM))
    yb = _moe(xs, block_expert, n_used.reshape(1), w_gu, b_gu, w_down, b_down, bm,
              min(MOE_TH, w_down.shape[1]))
    y_p, y_s = _combine(x1, gates, dest, yb, n_p, _tile(row_gcd, COMBINE_TM))

    return (y_p.reshape(bp, s, d), y_s.reshape(bs, t, d),
            k_p.reshape(bp, s, DA_HEADS, 2, DA_QK_DIM), v_p.reshape(bp, s, DA_HEADS, DA_V_DIM),
            s_all[0:1],
            k_s.reshape(bs, t, DA_HEADS, 2, DA_QK_DIM), v_s.reshape(bs, t, DA_HEADS, DA_V_DIM),
            s_all[1:])


def kernel(x_prompt, x_sample, cache_k, cache_v, state_hgrn, attn_norm_g, w_in, hg_lb_logits,
           hg_norm_g, da_q_norm_g, da_k_norm_g, da_lambda, da_subln_g, w_out, ffn_norm_g,
           w_router, b_router, w_gu, b_gu, w_down, b_down):
    depth = w_in.shape[0]
    assert depth == 1, "single-layer step"
    lb_all = jnp.cumsum(jax.nn.softmax(hg_lb_logits.astype(F32), axis=0), axis=0)
    lam_init = 0.8 - 0.6 * math.exp(-0.3 * 0)
    outs = _layer(x_prompt, x_sample, cache_k[0], cache_v[0], state_hgrn[0], attn_norm_g[0],
                  w_in[0], lb_all[0], hg_norm_g[0], da_q_norm_g[0], da_k_norm_g[0], da_lambda[0],
                  da_subln_g[0], w_out[0], ffn_norm_g[0], w_router[0], b_router[0], w_gu[0],
                  b_gu[0], w_down[0], b_down[0], lam_init)
    y_p, y_s, k_p, v_p, s_p, k_s, v_s, s_s = outs
    return (y_p, y_s, k_p[None], v_p[None], s_p[None], k_s[None], v_s[None], s_s[None])
```

```python
import functools
import math

import jax
import jax.numpy as jnp
from jax import lax
from jax.experimental import pallas as pl
from jax.experimental.pallas import tpu as pltpu

F32 = jnp.float32
BF16 = jnp.bfloat16
I32 = jnp.int32
U32 = jnp.uint32

RMS_EPS = 1e-6
CHUNK = 64
HG_HEADS = 8
HG_DIM = 128
DA_HEADS = 8
DA_QK_DIM = 64
DA_V_DIM = 128
HEAD_WIDTH = 1024
N_EXPERTS = 32
TOP_K = 4
SWIGLU_LIMIT = 7.0
SWIGLU_ALPHA = 1.702
LANES = 128
HG_TILE = 64
HG_SUB = 16
EXP_CLAMP = 80.0
LOG2_E = 1.4426950408889634
V_ROWS = DA_V_DIM + 16
ATTN_SAFE_BOUND = 50.0
VMEM_LIMIT = 56 * 1024 * 1024
PROJ_TM, PROJ_TN = 256, 3584
DA_TM = 256
ATTN_TILE = 512
MIX_TM = 256
DISPATCH_TM = 256
MOE_BM, MOE_TH = 1024, 256
COMBINE_TM = 128


def _cparams(sem, vmem=VMEM_LIMIT):
    return pltpu.CompilerParams(dimension_semantics=sem, vmem_limit_bytes=vmem)


def _dot(a, b):
    return jnp.dot(a, b, preferred_element_type=F32)


def _dot_nt(a, b):
    return lax.dot_general(a, b, (((1,), (1,)), ((), ())), preferred_element_type=F32)


def _split_bf16(x):
    hi = x.astype(BF16)
    lo = (x - hi.astype(F32)).astype(BF16)
    return hi, lo


def _sigmoid(x):
    return 1.0 / (1.0 + jnp.exp(-x))


def _group_sum(sq, gmat):
    outs = []
    for c in range(sq.shape[1] // 256):
        hi, lo = _split_bf16(sq[:, c * 256:(c + 1) * 256])
        outs.append(_dot(hi, gmat) + _dot(lo, gmat))
    return jnp.concatenate(outs, axis=1)


def _proj_kernel(n_prompt_tiles, xp_ref, xs_ref, g_ref, w_ref, o_ref):
    is_prompt = pl.program_id(1) < n_prompt_tiles
    x = jnp.where(is_prompt, xp_ref[...], xs_ref[...])
    ms = jnp.mean(x * x, axis=-1, keepdims=True)
    h = ((x * lax.rsqrt(ms + RMS_EPS)) * g_ref[...]).astype(BF16)
    o_ref[...] = _dot(h, w_ref[...])


def _proj(x_p, x_s, g, w_bf16, tm, tn):
    d = x_p.shape[1]
    n = x_p.shape[0] + x_s.shape[0]
    npt = x_p.shape[0] // tm
    cols = w_bf16.shape[1]
    return pl.pallas_call(
        functools.partial(_proj_kernel, npt),
        grid=(cols // tn, n // tm),
        in_specs=[pl.BlockSpec((tm, d), lambda j, i: (jnp.minimum(i, npt - 1), 0)),
                  pl.BlockSpec((tm, d), lambda j, i: (jnp.maximum(i - npt, 0), 0)),
                  pl.BlockSpec((1, d), lambda j, i: (0, 0)),
                  pl.BlockSpec((d, tn), lambda j, i: (0, j))],
        out_specs=pl.BlockSpec((tm, tn), lambda j, i: (i, j)),
        out_shape=jax.ShapeDtypeStruct((n, cols), F32),
        compiler_params=_cparams(("arbitrary", "arbitrary")),
        name="proj",
    )(x_p, x_s, g, w_bf16)


def _hgrn_kernel(n_prompt_tiles, q_ref, f_ref, i_ref, g_ref, lb_ref, ng_ref, s0_ref,
                 o_ref, sout_ref, st_scr):
    t = pl.program_id(0)
    starts = jnp.logical_or(t == 0, t >= n_prompt_tiles)
    ends = t >= n_prompt_tiles - 1

    @pl.when(starts)
    def _():
        for h in range(HG_HEADS):
            st_scr[h] = s0_ref[0, h].T

    lb = lb_ref[...]
    q = q_ref[...]
    q = q * _sigmoid(q)
    f = lb + (1.0 - lb) * _sigmoid(f_ref[...])
    logf = jnp.log(f)
    k = 1.0 - f
    v = i_ref[...]
    gate = _sigmoid(g_ref[...])

    row = lax.broadcasted_iota(I32, (HG_TILE, HG_TILE), 0)
    col = lax.broadcasted_iota(I32, (HG_TILE, HG_TILE), 1)
    causal = col <= row
    tri = causal.astype(BF16)
    hi, lo = _split_bf16(logf)
    b = _dot(tri, hi) + _dot(tri, lo)
    b_last = b[HG_TILE - 1:HG_TILE, :]
    q_dec = q * jnp.exp(b)
    k_dec = k * jnp.exp(b_last - b)
    ng = ng_ref[...]

    outs = []
    for h in range(HG_HEADS):
        sl = slice(h * HG_DIM, (h + 1) * HG_DIM)
        st = st_scr[h]
        bh = b[:, sl]
        qh = q[:, sl]
        kh = k[:, sl]
        vh = v[:, sl].astype(BF16)
        o = _dot_nt(q_dec[:, sl].astype(BF16), st.astype(BF16))
        rows = []
        for i in range(HG_TILE // HG_SUB):
            r0 = i * HG_SUB
            base = bh[r0 - 1:r0, :] if i else jnp.zeros((1, HG_DIM), F32)
            qi = qh[r0:r0 + HG_SUB, :] * jnp.exp(bh[r0:r0 + HG_SUB, :] - base)
            ki = kh * jnp.exp(jnp.minimum(base - bh, EXP_CLAMP))
            rows.append(_dot_nt(qi.astype(BF16), ki.astype(BF16)))
        scores = jnp.where(causal, jnp.concatenate(rows, axis=0), 0.0)
        o = o + _dot(scores.astype(BF16), vh)
        st_new = st * jnp.exp(b_last[:, sl]) + _dot(v[:, sl].T.astype(BF16),
                                                    k_dec[:, sl].astype(BF16))
        st_scr[h] = st_new
        ms = jnp.mean(o * o, axis=-1, keepdims=True)
        outs.append((o * lax.rsqrt(ms + RMS_EPS)) * ng)
    o_ref[...] = (jnp.concatenate(outs, axis=1) * gate).astype(o_ref.dtype)

    @pl.when(ends)
    def _():
        for h in range(HG_HEADS):
            sout_ref[0, h] = st_scr[h].T


def _hgrn(proj, lb, norm_g, s0_all, n_prompt_tiles):
    n = proj.shape[0]
    n_tiles = n // HG_TILE
    seq = lambda t: (jnp.maximum(t - (n_prompt_tiles - 1), 0), 0, 0, 0)
    col_spec = lambda c: pl.BlockSpec((HG_TILE, HEAD_WIDTH), lambda t, c=c: (t, c))
    return pl.pallas_call(
        functools.partial(_hgrn_kernel, n_prompt_tiles),
        grid=(n_tiles,),
        in_specs=[col_spec(0), col_spec(1), col_spec(2), col_spec(3),
                  pl.BlockSpec((1, HEAD_WIDTH), lambda t: (0, 0)),
                  pl.BlockSpec((1, HG_DIM), lambda t: (0, 0)),
                  pl.BlockSpec((1, HG_HEADS, HG_DIM, HG_DIM), seq)],
        out_specs=[pl.BlockSpec((HG_TILE, HEAD_WIDTH), lambda t: (t, 0)),
                   pl.BlockSpec((1, HG_HEADS, HG_DIM, HG_DIM), seq)],
        out_shape=[jax.ShapeDtypeStruct((n, HEAD_WIDTH), BF16),
                   jax.ShapeDtypeStruct(s0_all.shape, F32)],
        scratch_shapes=[pltpu.VMEM((HG_HEADS, HG_DIM, HG_DIM), F32)],
        compiler_params=_cparams(("arbitrary",)),
        name="hgrn",
    )(proj, proj, proj, proj, lb, norm_g, s0_all)


def _da_prep_kernel(n_prompt_tiles, dq_ref, dk_ref, dv_ref, gq_ref, gk_ref, gmat_ref,
                    qlo_ref, qhi_ref, kb_ref, vt_ref, kp_ref, ks_ref, vp_ref, vs_ref):
    i = pl.program_id(0)
    gmat = gmat_ref[...]
    inv = 1.0 / DA_QK_DIM

    dq = dq_ref[...]
    qn = dq * lax.rsqrt(_group_sum(dq * dq, gmat) * inv + RMS_EPS) * gq_ref[...]
    qn = qn * (DA_QK_DIM ** -0.5 * LOG2_E)
    lane = lax.broadcasted_iota(I32, qn.shape, 1)
    first = (lane & DA_QK_DIM) == 0
    qlo_ref[...] = jnp.where(first, qn, 0.0).astype(BF16)
    qhi_ref[...] = jnp.where(first, 0.0, qn).astype(BF16)

    dk = dk_ref[...]
    kn = dk * lax.rsqrt(_group_sum(dk * dk, gmat) * inv + RMS_EPS) * gk_ref[...]
    kb_ref[...] = kn.astype(BF16)
    v = dv_ref[...]
    vt = v.T
    tail = (lax.broadcasted_iota(I32, (V_ROWS - DA_V_DIM, vt.shape[1]), 0) == 0).astype(F32)
    pieces = []
    for h in range(DA_HEADS):
        pieces += [vt[h * DA_V_DIM:(h + 1) * DA_V_DIM, :], tail]
    vt_ref[...] = jnp.concatenate(pieces, axis=0).astype(BF16)

    @pl.when(i < n_prompt_tiles)
    def _():
        kp_ref[...] = kn
        vp_ref[...] = v

    @pl.when(i >= n_prompt_tiles)
    def _():
        ks_ref[...] = kn
        vs_ref[...] = v


def _da_prep(proj, gq, gk, gmat, n_prompt, tm):
    n = proj.shape[0]
    npt = n_prompt // tm
    full = lambda: pl.BlockSpec((tm, HEAD_WIDTH), lambda i: (i, 0))
    p_spec = lambda: pl.BlockSpec((tm, HEAD_WIDTH), lambda i: (jnp.minimum(i, npt - 1), 0))
    s_spec = lambda: pl.BlockSpec((tm, HEAD_WIDTH), lambda i: (jnp.maximum(i - npt, 0), 0))
    vec = lambda: pl.BlockSpec((1, HEAD_WIDTH), lambda i: (0, 0))
    return pl.pallas_call(
        functools.partial(_da_prep_kernel, npt),
        grid=(n // tm,),
        in_specs=[pl.BlockSpec((tm, HEAD_WIDTH), lambda i: (i, 4)),
                  pl.BlockSpec((tm, HEAD_WIDTH), lambda i: (i, 5)),
                  pl.BlockSpec((tm, HEAD_WIDTH), lambda i: (i, 6)),
                  vec(), vec(),
                  pl.BlockSpec((256, 256), lambda i: (0, 0))],
        out_specs=[full(), full(), full(),
                   pl.BlockSpec((DA_HEADS * V_ROWS, tm), lambda i: (0, i)),
                   p_spec(), s_spec(), p_spec(), s_spec()],
        out_shape=[jax.ShapeDtypeStruct((n, HEAD_WIDTH), BF16),
                   jax.ShapeDtypeStruct((n, HEAD_WIDTH), BF16),
                   jax.ShapeDtypeStruct((n, HEAD_WIDTH), BF16),
                   jax.ShapeDtypeStruct((DA_HEADS * V_ROWS, n), BF16),
                   jax.ShapeDtypeStruct((n_prompt, HEAD_WIDTH), F32),
                   jax.ShapeDtypeStruct((n - n_prompt, HEAD_WIDTH), F32),
                   jax.ShapeDtypeStruct((n_prompt, HEAD_WIDTH), F32),
                   jax.ShapeDtypeStruct((n - n_prompt, HEAD_WIDTH), F32)],
        compiler_params=_cparams(("arbitrary",)),
        name="da_prep",
    )(proj, proj, proj, gq, gk, gmat)


def _lambda_of(lp, lam_init):
    a = jnp.sum(lp[0:1, :] * lp[1:2, :], axis=-1, keepdims=True)
    b = jnp.sum(lp[2:3, :] * lp[3:4, :], axis=-1, keepdims=True)
    return jnp.exp(a) - jnp.exp(b) + lam_init


def _subln(o, g, lam_init):
    ms = jnp.mean(o * o, axis=-1, keepdims=True)
    return (o * lax.rsqrt(ms + RMS_EPS)) * g * (1.0 - lam_init)


def _attn_p_kernel(lam_init, tile, qi_ref, kj_ref, bounded_ref, qlo_ref, qhi_ref, k_ref, vt_ref,
                   bound_ref, lp_ref, sg_ref, o_ref, m_scr, acc_scr):
    s = pl.program_id(0)
    qi = qi_ref[s]
    kj = kj_ref[s]
    bounded = bounded_ref[0] != 0
    n_maps = 2 * DA_HEADS

    @pl.when(kj == 0)
    def _():
        m_scr[...] = jnp.full(m_scr.shape, -jnp.inf, F32)
        acc_scr[...] = jnp.zeros(acc_scr.shape, F32)

    def scores(r, visible):
        h, c = divmod(r, 2)
        sl = slice(h * DA_V_DIM, (h + 1) * DA_V_DIM)
        sc = _dot_nt(k_ref[:, sl], (qlo_ref, qhi_ref)[c][:, sl])
        return sc if visible is None else jnp.where(visible, sc, -jnp.inf)

    def sweep(visible):
        sc = scores(0, visible)
        for r in range(n_maps):
            nxt = scores(r + 1, visible) if r + 1 < n_maps else None
            h = r // 2
            m_prev = m_scr[r:r + 1, :]
            m_new = jnp.maximum(m_prev, jnp.max(sc, axis=0, keepdims=True))
            alpha = jnp.exp2(m_prev - m_new)
            e = jnp.exp2(sc - m_new).astype(BF16)
            acc_scr[r] = alpha * acc_scr[r] + _dot(vt_ref[h * V_ROWS:(h + 1) * V_ROWS, :], e)
            m_scr[r:r + 1, :] = m_new
            sc = nxt

    def sweep_bounded(visible):
        bound = bound_ref[0:1, 0:1]
        for r in range(n_maps):
            h = r // 2
            e = jnp.exp2(scores(r, None) - bound)
            if visible is not None:
                e = jnp.where(visible, e, 0.0)
            acc_scr[r] += _dot(vt_ref[h * V_ROWS:(h + 1) * V_ROWS, :], e.astype(BF16))

    def chunk_mask():
        shift = CHUNK.bit_length() - 1
        key_chunk = lax.broadcasted_iota(I32, (tile, tile), 0) >> shift
        qry_chunk = lax.broadcasted_iota(I32, (tile, tile), 1) >> shift
        return key_chunk <= qry_chunk

    off_diag = kj < qi
    on_diag = kj == qi
    unbounded = jnp.logical_not(bounded)

    @pl.when(jnp.logical_and(off_diag, bounded))
    def _():
        sweep_bounded(None)

    @pl.when(jnp.logical_and(on_diag, bounded))
    def _():
        sweep_bounded(chunk_mask())

    @pl.when(jnp.logical_and(off_diag, unbounded))
    def _():
        sweep(None)

    @pl.when(jnp.logical_and(on_diag, unbounded))
    def _():
        sweep(chunk_mask())

    @pl.when(on_diag)
    def _():
        lam = _lambda_of(lp_ref[...], lam_init)
        sg = sg_ref[...]
        outs = []
        for h in range(DA_HEADS):
            acc0 = acc_scr[2 * h]
            acc1 = acc_scr[2 * h + 1]
            a0 = acc0[0:DA_V_DIM, :] / acc0[DA_V_DIM:DA_V_DIM + 1, :]
            a1 = acc1[0:DA_V_DIM, :] / acc1[DA_V_DIM:DA_V_DIM + 1, :]
            outs.append(_subln((a0 - lam * a1).T, sg, lam_init))
        o_ref[...] = jnp.concatenate(outs, axis=1).astype(o_ref.dtype)


def _attn_prompt(qlo, qhi, kb, vt, bound, lp, sg, n_prompt, lam_init, tile):
    nq = n_prompt // tile
    qi_l, kj_l = [], []
    for i in range(nq):
        for j in range(i + 1):
            qi_l.append(i)
            kj_l.append(j)
    qi = jnp.asarray(qi_l, I32)
    kj = jnp.asarray(kj_l, I32)
    bounded = (bound <= ATTN_SAFE_BOUND).astype(I32).reshape(1)
    bound_row = jnp.broadcast_to(bound.astype(F32), (1, LANES))
    const = lambda shp: pl.BlockSpec(shp, lambda s, qi, kj, bd: (0, 0))
    grid_spec = pltpu.PrefetchScalarGridSpec(
        num_scalar_prefetch=3,
        grid=(len(qi_l),),
        in_specs=[pl.BlockSpec((tile, HEAD_WIDTH), lambda s, qi, kj, bd: (qi[s], 0)),
                  pl.BlockSpec((tile, HEAD_WIDTH), lambda s, qi, kj, bd: (qi[s], 0)),
                  pl.BlockSpec((tile, HEAD_WIDTH), lambda s, qi, kj, bd: (kj[s], 0)),
                  pl.BlockSpec((DA_HEADS * V_ROWS, tile), lambda s, qi, kj, bd: (0, kj[s])),
                  const((1, LANES)), const((4, DA_QK_DIM)), const((1, DA_V_DIM))],
        out_specs=pl.BlockSpec((tile, HEAD_WIDTH), lambda s, qi, kj, bd: (qi[s], 0)),
        scratch_shapes=[pltpu.VMEM((2 * DA_HEADS, tile), F32),
                        pltpu.VMEM((2 * DA_HEADS, V_ROWS, tile), F32)])
    return pl.pallas_call(
        functools.partial(_attn_p_kernel, lam_init, tile),
        grid_spec=grid_spec,
        out_shape=jax.ShapeDtypeStruct((n_prompt, HEAD_WIDTH), BF16),
        compiler_params=_cparams(("arbitrary",)),
        name="attn_p",
    )(qi, kj, bounded, qlo, qhi, kb, vt, bound_row, lp, sg)


def _attn_s_kernel(lam_init, qlo_ref, qhi_ref, kn_ref, vn_ref, ckt_ref, cv_ref, lp_ref, sg_ref,
                   o_ref):
    past = ckt_ref.shape[-1]
    lam = _lambda_of(lp_ref[...], lam_init)
    sg = sg_ref[...]
    outs = []
    for h in range(DA_HEADS):
        sl = slice(h * DA_V_DIM, (h + 1) * DA_V_DIM)
        kn = kn_ref[:, sl].astype(BF16)
        vc = cv_ref[0, pl.ds(h, past, stride=DA_HEADS), :].astype(BF16)
        vn = vn_ref[:, sl].astype(BF16)
        acc = None
        for c, q_ref in enumerate((qlo_ref, qhi_ref)):
            qh = q_ref[:, sl]
            kct = ckt_ref[0, h, c].astype(BF16)
            s_c = _dot(qh[:, c * DA_QK_DIM:(c + 1) * DA_QK_DIM], kct)
            s_n = _dot_nt(qh, kn)
            m = jnp.maximum(jnp.max(s_c, axis=-1, keepdims=True),
                            jnp.max(s_n, axis=-1, keepdims=True))
            e_c = jnp.exp2(s_c - m)
            e_n = jnp.exp2(s_n - m)
            l = jnp.sum(e_c, axis=-1, keepdims=True) + jnp.sum(e_n, axis=-1, keepdims=True)
            part = (_dot(e_c.astype(BF16), vc) + _dot(e_n.astype(BF16), vn)) / l
            acc = part if c == 0 else acc - lam * part
        outs.append(_subln(acc, sg, lam_init))
    o_ref[...] = jnp.concatenate(outs, axis=1).astype(o_ref.dtype)


def _attn_sample(qlo, qhi, ks, vs, cache_k, cache_v, lp, sg, n_prompt, lam_init):
    bs, past = cache_k.shape[0], cache_k.shape[1]
    cache_kt = jnp.transpose(cache_k, (0, 2, 3, 4, 1))
    cache_v = cache_v.reshape(bs, past * DA_HEADS, DA_V_DIM)
    t = ks.shape[0] // bs
    off = n_prompt // t
    return pl.pallas_call(
        functools.partial(_attn_s_kernel, lam_init),
        grid=(bs,),
        in_specs=[pl.BlockSpec((t, HEAD_WIDTH), lambda b: (off + b, 0)),
                  pl.BlockSpec((t, HEAD_WIDTH), lambda b: (off + b, 0)),
                  pl.BlockSpec((t, HEAD_WIDTH), lambda b: (b, 0)),
                  pl.BlockSpec((t, HEAD_WIDTH), lambda b: (b, 0)),
                  pl.BlockSpec((1, DA_HEADS, 2, DA_QK_DIM, past), lambda b: (b, 0, 0, 0, 0)),
                  pl.BlockSpec((1, past * DA_HEADS, DA_V_DIM), lambda b: (b, 0, 0)),
                  pl.BlockSpec((4, DA_QK_DIM), lambda b: (0, 0)),
                  pl.BlockSpec((1, DA_V_DIM), lambda b: (0, 0))],
        out_specs=pl.BlockSpec((t, HEAD_WIDTH), lambda b: (b, 0)),
        out_shape=jax.ShapeDtypeStruct((bs * t, HEAD_WIDTH), BF16),
        compiler_params=_cparams(("arbitrary",)),
        name="attn_s",
    )(qlo, qhi, ks, vs, cache_kt, cache_v, lp, sg)


def _pack_bf16_pairs(x):
    half = x.shape[1] // 2
    lo = lax.bitcast_convert_type(x[:, :half].astype(BF16).astype(F32), U32)
    hi = lax.bitcast_convert_type(x[:, half:].astype(BF16).astype(F32), U32)
    return (lo >> 16) | (hi & jnp.uint32(0xFFFF0000))


def _unpack_bf16_pairs(w):
    lo = lax.bitcast_convert_type(w << 16, F32).astype(BF16)
    hi = lax.bitcast_convert_type(w & jnp.uint32(0xFFFF0000), F32).astype(BF16)
    return lo, hi


def _mix_kernel(n_prompt_tiles, hg_ref, dap_ref, das_ref, xp_ref, xs_ref, wo_ref, g2_ref, wr_ref,
                br_ref, x1_ref, h2p_ref, route_ref, gates_ref, cnt_ref, base_scr):
    i = pl.program_id(0)

    @pl.when(i == 0)
    def _():
        base_scr[...] = jnp.zeros(base_scr.shape, F32)

    tm = xp_ref.shape[0]
    is_prompt = i < n_prompt_tiles
    da = jnp.where(is_prompt, dap_ref[...], das_ref[...])
    x = jnp.where(is_prompt, xp_ref[...], xs_ref[...])
    x1 = x + _dot(hg_ref[...], wo_ref[0:HEAD_WIDTH, :]) \
        + _dot(da, wo_ref[HEAD_WIDTH:2 * HEAD_WIDTH, :])
    x1_ref[...] = x1
    ms = jnp.mean(x1 * x1, axis=-1, keepdims=True)
    h2 = (x1 * lax.rsqrt(ms + RMS_EPS)) * g2_ref[...]
    h2p_ref[...] = _pack_bf16_pairs(h2)

    h_hi, h_lo = _split_bf16(h2)
    w_hi, w_lo = _split_bf16(wr_ref[...])
    logits = _dot(h_hi, w_hi) + _dot(h_lo, w_hi) + _dot(h_hi, w_lo) + br_ref[...]
    lane = lax.broadcasted_iota(I32, logits.shape, 1)
    lanef = lane.astype(F32)
    lg = jnp.where(lane < N_EXPERTS, logits, -jnp.inf)

    sels, vals, idxs = [], [], []
    for _ in range(TOP_K):
        mx = jnp.max(lg, axis=-1, keepdims=True)
        idx = jnp.min(jnp.where(lg == mx, lanef, float(LANES)), axis=-1, keepdims=True)
        sel = lanef == idx
        lg = jnp.where(sel, -jnp.inf, lg)
        sels.append(sel)
        vals.append(mx)
        idxs.append(idx)
    es = [jnp.exp(v - vals[0]) for v in vals]
    den = es[0] + es[1] + es[2] + es[3]

    onehot = jnp.zeros(logits.shape, F32)
    for sel in sels:
        onehot = onehot + sel.astype(F32)
    row = lax.broadcasted_iota(I32, (tm, tm), 0)
    col = lax.broadcasted_iota(I32, (tm, tm), 1)
    strict = (col < row).astype(BF16)
    before = _dot(strict, onehot.astype(BF16)) + base_scr[0:1, :]
    base_scr[0:1, :] = base_scr[0:1, :] + jnp.sum(onehot, axis=0, keepdims=True)

    route = jnp.zeros(logits.shape, F32)
    gates = jnp.zeros(logits.shape, F32)
    for r in range(TOP_K):
        rank = jnp.sum(jnp.where(sels[r], before, 0.0), axis=-1, keepdims=True)
        route = jnp.where(lane == r, idxs[r], route)
        route = jnp.where(lane == TOP_K + r, rank, route)
        gates = jnp.where(lane == r, es[r] / den, gates)
    route_ref[...] = route.astype(I32)
    gates_ref[...] = gates
    cnt_ref[...] = jnp.broadcast_to(base_scr[0:1, :], cnt_ref.shape).astype(I32)


def _mix(hg, da_p, da_s, x_p, x_s, wo_bf16, g2, wr_pad, br_pad, tm):
    d = x_p.shape[1]
    n = x_p.shape[0] + x_s.shape[0]
    npt = x_p.shape[0] // tm
    row = lambda w: pl.BlockSpec((tm, w), lambda i: (i, 0))
    p_row = lambda w: pl.BlockSpec((tm, w), lambda i: (jnp.minimum(i, npt - 1), 0))
    s_row = lambda w: pl.BlockSpec((tm, w), lambda i: (jnp.maximum(i - npt, 0), 0))
    const = lambda shp: pl.BlockSpec(shp, lambda i: (0, 0))
    return pl.pallas_call(
        functools.partial(_mix_kernel, npt),
        grid=(n // tm,),
        in_specs=[row(HEAD_WIDTH), p_row(HEAD_WIDTH), s_row(HEAD_WIDTH), p_row(d), s_row(d),
                  const((2 * HEAD_WIDTH, d)), const((1, d)), const((d, LANES)), const((1, LANES))],
        out_specs=[row(d), row(d // 2), row(LANES), row(LANES), const((8, LANES))],
        out_shape=[jax.ShapeDtypeStruct((n, d), F32),
                   jax.ShapeDtypeStruct((n, d // 2), U32),
                   jax.ShapeDtypeStruct((n, LANES), I32),
                   jax.ShapeDtypeStruct((n, LANES), F32),
                   jax.ShapeDtypeStruct((8, LANES), I32)],
        scratch_shapes=[pltpu.VMEM((8, LANES), F32)],
        compiler_params=_cparams(("arbitrary",)),
        name="mix",
    )(hg, da_p, da_s, x_p, x_s, wo_bf16, g2, wr_pad, br_pad)


def _dispatch_copy(h_ref, xs_ref, sem, dest_ref, r, kk):
    return pltpu.make_async_copy(h_ref.at[pl.ds(r, 1)],
                                 xs_ref.at[pl.ds(dest_ref[r * TOP_K + kk], 1)], sem)


def _dispatch_kernel(zinfo_ref, dest_ref, h_ref, xs_ref, zbuf, sem, zsem):
    tm = h_ref.shape[0]
    bm = zbuf.shape[0]
    nb = xs_ref.shape[0] // bm

    def zero_copy(start):
        return pltpu.make_async_copy(zbuf, xs_ref.at[pl.ds(pl.multiple_of(start, bm), bm)], zsem)

    def for_each_zero_block(fn):
        for e in range(N_EXPERTS):
            start = zinfo_ref[e]

            @pl.when(start >= 0)
            def _():
                fn(zero_copy(start))

        def tail(b, carry):
            fn(zero_copy(b * bm))
            return carry

        lax.fori_loop(zinfo_ref[N_EXPERTS], nb, tail, 0)

    @pl.when(pl.program_id(0) == 0)
    def _():
        zbuf[...] = jnp.zeros(zbuf.shape, zbuf.dtype)
        for_each_zero_block(lambda cp: cp.start())
        for_each_zero_block(lambda cp: cp.wait())

    def issue(r, carry):
        for kk in range(TOP_K):
            _dispatch_copy(h_ref, xs_ref, sem, dest_ref, r, kk).start()
        return carry

    def drain(r, carry):
        for kk in range(TOP_K):
            _dispatch_copy(h_ref, xs_ref, sem, dest_ref, r, kk).wait()
        return carry

    lax.fori_loop(0, tm, issue, 0)
    lax.fori_loop(0, tm, drain, 0)


def _dispatch(h2p, dest_flat, zinfo, n_rows, bm, tm):
    n, w = h2p.shape
    grid_spec = pltpu.PrefetchScalarGridSpec(
        num_scalar_prefetch=1,
        grid=(n // tm,),
        in_specs=[pl.BlockSpec((tm * TOP_K,), lambda i, z: (i,), memory_space=pltpu.SMEM),
                  pl.BlockSpec((tm, w), lambda i, z: (i, 0))],
        out_specs=pl.BlockSpec(memory_space=pl.ANY),
        scratch_shapes=[pltpu.VMEM((bm, w), h2p.dtype),
                        pltpu.SemaphoreType.DMA, pltpu.SemaphoreType.DMA])
    return pl.pallas_call(
        _dispatch_kernel,
        grid_spec=grid_spec,
        out_shape=jax.ShapeDtypeStruct((n_rows, w), h2p.dtype),
        compiler_params=_cparams(("arbitrary",)),
        name="dispatch",
    )(zinfo, dest_flat, h2p)


def _moe_kernel(be_ref, nu_ref, xs_ref, wg_ref, wu_ref, bg_ref, bu_ref, wd_ref, bd_ref,
                o_ref, xb_scr):
    i = pl.program_id(0)
    j = pl.program_id(1)
    used = i < nu_ref[0]

    @pl.when(jnp.logical_and(j == 0, used))
    def _():
        half = xs_ref.shape[1]
        lo, hi = _unpack_bf16_pairs(xs_ref[...])
        xb_scr[:, 0:half] = lo
        xb_scr[:, half:2 * half] = hi
        o_ref[...] = jnp.broadcast_to(bd_ref[0], o_ref.shape)

    @pl.when(jnp.logical_and(j == 0, jnp.logical_not(used)))
    def _():
        o_ref[...] = jnp.zeros(o_ref.shape, F32)

    @pl.when(used)
    def _():
        xb = xb_scr[...]
        gate = _dot(xb, wg_ref[0].astype(BF16)) + bg_ref[0]
        up = _dot(xb, wu_ref[0].astype(BF16)) + bu_ref[0]
        gate = jnp.minimum(gate, SWIGLU_LIMIT)
        up = jnp.clip(up, -SWIGLU_LIMIT, SWIGLU_LIMIT)
        glu = gate * _sigmoid(SWIGLU_ALPHA * gate)
        mid = ((up + 1.0) * glu).astype(BF16)
        o_ref[...] += _dot(mid, wd_ref[0].astype(BF16))


def _moe(xs, block_expert, n_used, w_gu, b_gu, w_down, b_down, bm, th):
    rows, half = xs.shape
    d = 2 * half
    n_e, _, two_h = w_gu.shape
    hid = two_h // 2
    nj = hid // th
    nb = rows // bm
    b_gu3 = b_gu.reshape(n_e, 1, two_h)
    b_dn3 = b_down.reshape(n_e, 1, d)

    def xrow(i, j, be, nu):
        return (jnp.minimum(i, jnp.maximum(nu[0] - 1, 0)), 0)

    def hcol(i, j, nu):
        return jnp.where(i < nu[0], j, nj - 1)

    grid_spec = pltpu.PrefetchScalarGridSpec(
        num_scalar_prefetch=2,
        grid=(nb, nj),
        in_specs=[pl.BlockSpec((bm, half), xrow),
                  pl.BlockSpec((1, d, th), lambda i, j, be, nu: (be[i], 0, hcol(i, j, nu))),
                  pl.BlockSpec((1, d, th), lambda i, j, be, nu: (be[i], 0, nj + hcol(i, j, nu))),
                  pl.BlockSpec((1, 1, th), lambda i, j, be, nu: (be[i], 0, hcol(i, j, nu))),
                  pl.BlockSpec((1, 1, th), lambda i, j, be, nu: (be[i], 0, nj + hcol(i, j, nu))),
                  pl.BlockSpec((1, th, d), lambda i, j, be, nu: (be[i], hcol(i, j, nu), 0)),
                  pl.BlockSpec((1, 1, d), lambda i, j, be, nu: (be[i], 0, 0))],
        out_specs=pl.BlockSpec((bm, d), lambda i, j, be, nu: (i, 0)),
        scratch_shapes=[pltpu.VMEM((bm, d), BF16)])
    return pl.pallas_call(
        _moe_kernel,
        grid_spec=grid_spec,
        out_shape=jax.ShapeDtypeStruct((rows, d), F32),
        compiler_params=_cparams(("arbitrary", "arbitrary")),
        name="moe",
    )(block_expert, n_used, xs, w_gu, w_gu, b_gu3, b_gu3, w_down, b_dn3)


def _combine_kernel(n_prompt_tiles, n_tiles, dest_ref, dest_next_ref, x1_ref, gates_ref, yb_ref,
                    yp_ref, ys_ref, buf, sem):
    i = pl.program_id(0)
    tm = x1_ref.shape[0]
    slot = i % 2

    def for_each_row_copy(d_ref, s, fn):
        def body(r, carry):
            for kk in range(TOP_K):
                fn(pltpu.make_async_copy(yb_ref.at[pl.ds(d_ref[r * TOP_K + kk], 1)],
                                         buf.at[s, kk, pl.ds(r, 1)], sem.at[s]))
            return carry

        lax.fori_loop(0, tm, body, 0)

    @pl.when(i == 0)
    def _():
        for_each_row_copy(dest_ref, 0, lambda cp: cp.start())

    @pl.when(i + 1 < n_tiles)
    def _():
        for_each_row_copy(dest_next_ref, 1 - slot, lambda cp: cp.start())

    for_each_row_copy(dest_ref, slot, lambda cp: cp.wait())

    g = gates_ref[...]
    y = x1_ref[...]
    for kk in range(TOP_K):
        y = y + g[:, kk:kk + 1] * buf[slot, kk]

    @pl.when(i < n_prompt_tiles)
    def _():
        yp_ref[...] = y

    @pl.when(i >= n_prompt_tiles)
    def _():
        ys_ref[...] = y


def _combine(x1, gates, dest_flat, yb, n_prompt, tm):
    n, d = x1.shape
    npt = n_prompt // tm
    n_tiles = n // tm
    return pl.pallas_call(
        functools.partial(_combine_kernel, npt, n_tiles),
        grid=(n_tiles,),
        in_specs=[pl.BlockSpec((tm * TOP_K,), lambda i: (i,), memory_space=pltpu.SMEM),
                  pl.BlockSpec((tm * TOP_K,), lambda i: (jnp.minimum(i + 1, n_tiles - 1),),
                               memory_space=pltpu.SMEM),
                  pl.BlockSpec((tm, d), lambda i: (i, 0)),
                  pl.BlockSpec((tm, LANES), lambda i: (i, 0)),
                  pl.BlockSpec(memory_space=pl.ANY)],
        out_specs=[pl.BlockSpec((tm, d), lambda i: (jnp.minimum(i, npt - 1), 0)),
                   pl.BlockSpec((tm, d), lambda i: (jnp.maximum(i - npt, 0), 0))],
        out_shape=[jax.ShapeDtypeStruct((n_prompt, d), F32),
                   jax.ShapeDtypeStruct((n - n_prompt, d), F32)],
        scratch_shapes=[pltpu.VMEM((2, TOP_K, tm, d), F32), pltpu.SemaphoreType.DMA((2,))],
        compiler_params=_cparams(("arbitrary",)),
        name="combine",
    )(dest_flat, dest_flat, x1, gates, yb)


def _group_matrix(group):
    r = jnp.arange(256) // group
    return (r[:, None] == r[None, :]).astype(BF16)


def _tile(n, want):
    t = min(want, n)
    while n % t:
        t //= 2
    return t


def _layer(x_prompt, x_sample, cache_k, cache_v, state_hgrn, attn_norm_g, w_in, lb, hg_norm_g,
           da_q_norm_g, da_k_norm_g, da_lambda, da_subln_g, w_out, ffn_norm_g, w_router,
           b_router, w_gu, b_gu, w_down, b_down, lam_init):
    bp, s, d = x_prompt.shape
    bs, t, _ = x_sample.shape
    assert bp == 1 and t == HG_TILE and s % HG_TILE == 0
    n_p = bp * s
    n = n_p + bs * t
    x_p = x_prompt.reshape(n_p, d)
    x_s = x_sample.reshape(bs * t, d)
    row_gcd = math.gcd(n_p, bs * t)

    proj = _proj(x_p, x_s, attn_norm_g.reshape(1, d), w_in.astype(BF16), _tile(row_gcd, PROJ_TM),
                 PROJ_TN)

    s0_all = jnp.concatenate([jnp.zeros((1,) + state_hgrn.shape[1:], F32), state_hgrn], axis=0)
    hg, s_all = _hgrn(proj, lb.reshape(1, HEAD_WIDTH), hg_norm_g.reshape(1, HG_DIM), s0_all,
                      n_p // HG_TILE)

    tm_da = _tile(row_gcd, DA_TM)
    gq = jnp.tile(da_q_norm_g, 2 * DA_HEADS).reshape(1, HEAD_WIDTH)
    gk = jnp.tile(da_k_norm_g, 2 * DA_HEADS).reshape(1, HEAD_WIDTH)
    qlo, qhi, kb, vt, k_p, k_s, v_p, v_s = _da_prep(proj, gq, gk, _group_matrix(DA_QK_DIM),
                                                    n_p, tm_da)
    sg = da_subln_g.reshape(1, DA_V_DIM)
    score_bound = (DA_QK_DIM ** 0.5 * LOG2_E) * jnp.max(jnp.abs(da_q_norm_g)) \
        * jnp.max(jnp.abs(da_k_norm_g))
    da_p = _attn_prompt(qlo, qhi, kb, vt, score_bound, da_lambda, sg, n_p, lam_init,
                        _tile(n_p, ATTN_TILE))
    past = cache_k.shape[1]
    da_s = _attn_sample(qlo, qhi, k_s, v_s, cache_k, cache_v, da_lambda, sg, n_p, lam_init)

    wr_pad = jnp.pad(w_router, ((0, 0), (0, LANES - N_EXPERTS)))
    br_pad = jnp.pad(b_router, (0, LANES - N_EXPERTS)).reshape(1, LANES)
    x1, h2p, route, gates, cnt = _mix(hg, da_p, da_s, x_p, x_s, w_out.astype(BF16),
                                      ffn_norm_g.reshape(1, d), wr_pad, br_pad,
                                      _tile(row_gcd, MIX_TM))

    bm = MOE_BM
    counts = cnt[0, :N_EXPERTS]
    padded = (counts + bm - 1) // bm * bm
    pend = jnp.cumsum(padded)
    pstart = pend - padded
    dest = (pstart[route[:, 0:TOP_K]] + route[:, TOP_K:2 * TOP_K]).reshape(-1).astype(I32)
    nb = (n * TOP_K + N_EXPERTS * (bm - 1)) // bm
    n_used = (pend[-1] // bm).astype(I32)
    blk = jnp.minimum(jnp.arange(nb, dtype=I32), jnp.maximum(n_used - 1, 0))
    block_expert = jnp.minimum(jnp.sum(pend[None, :] <= (blk * bm)[:, None], axis=1),
                               N_EXPERTS - 1).astype(I32)
    zinfo = jnp.concatenate([jnp.where(padded > 0, pend - bm, -1), n_used[None]]).astype(I32)

    xs = _dispatch(h2p, dest, zinfo, nb * bm, bm, _tile(n, DISPATCH_TM))
    yb = _moe(xs, block_expert, n_used.reshape(1), w_gu, b_gu, w_down, b_down, bm,
              min(MOE_TH, w_down.shape[1]))
    y_p, y_s = _combine(x1, gates, dest, yb, n_p, _tile(row_gcd, COMBINE_TM))

    return (y_p.reshape(bp, s, d), y_s.reshape(bs, t, d),
            k_p.reshape(bp, s, DA_HEADS, 2, DA_QK_DIM), v_p.reshape(bp, s, DA_HEADS, DA_V_DIM),
            s_all[0:1],
            k_s.reshape(bs, t, DA_HEADS, 2, DA_QK_DIM), v_s.reshape(bs, t, DA_HEADS, DA_V_DIM),
            s_all[1:])


def kernel(x_prompt, x_sample, cache_k, cache_v, state_hgrn, attn_norm_g, w_in, hg_lb_logits,
           hg_norm_g, da_q_norm_g, da_k_norm_g, da_lambda, da_subln_g, w_out, ffn_norm_g,
           w_router, b_router, w_gu, b_gu, w_down, b_down):
    depth = w_in.shape[0]
    assert depth == 1, "single-layer step"
    lb_all = jnp.cumsum(jax.nn.softmax(hg_lb_logits.astype(F32), axis=0), axis=0)
    lam_init = 0.8 - 0.6 * math.exp(-0.3 * 0)
    outs = _layer(x_prompt, x_sample, cache_k[0], cache_v[0], state_hgrn[0], attn_norm_g[0],
                  w_in[0], lb_all[0], hg_norm_g[0], da_q_norm_g[0], da_k_norm_g[0], da_lambda[0],
                  da_subln_g[0], w_out[0], ffn_norm_g[0], w_router[0], b_router[0], w_gu[0],
                  b_gu[0], w_down[0], b_down[0], lam_init)
    y_p, y_s, k_p, v_p, s_p, k_s, v_s, s_s = outs
    return (y_p, y_s, k_p[None], v_p[None], s_p[None], k_s[None], v_s[None], s_s[None])
```

```python
import functools
import math

import jax
import jax.numpy as jnp
from jax import lax
from jax.experimental import pallas as pl
from jax.experimental.pallas import tpu as pltpu

F32 = jnp.float32
BF16 = jnp.bfloat16
I32 = jnp.int32
U32 = jnp.uint32

RMS_EPS = 1e-6
CHUNK = 64
HG_HEADS = 8
HG_DIM = 128
DA_HEADS = 8
DA_QK_DIM = 64
DA_V_DIM = 128
HEAD_WIDTH = 1024
N_EXPERTS = 32
TOP_K = 4
SWIGLU_LIMIT = 7.0
SWIGLU_ALPHA = 1.702
LANES = 128
HG_TILE = 64
HG_SUB = 16
HG_GROUP = 4
EXP_CLAMP = 80.0
LOG2_E = 1.4426950408889634
V_ROWS = DA_V_DIM + 16
ATTN_SAFE_BOUND = 50.0
VMEM_LIMIT = 56 * 1024 * 1024
PROJ_TM, PROJ_TN = 256, 3584
DA_TM = 256
ATTN_TILE = 512
MIX_TM = 256
DISPATCH_TM = 256
MOE_BM, MOE_TH = 1024, 256
COMBINE_TM = 128


def _cparams(sem, vmem=VMEM_LIMIT):
    return pltpu.CompilerParams(dimension_semantics=sem, vmem_limit_bytes=vmem)


def _dot(a, b):
    return jnp.dot(a, b, preferred_element_type=F32)


def _dot_nt(a, b):
    return lax.dot_general(a, b, (((1,), (1,)), ((), ())), preferred_element_type=F32)


def _split_bf16(x):
    hi = x.astype(BF16)
    lo = (x - hi.astype(F32)).astype(BF16)
    return hi, lo


def _sigmoid(x):
    return 1.0 / (1.0 + jnp.exp(-x))


def _group_sum(sq, gmat):
    outs = []
    for c in range(sq.shape[1] // 256):
        hi, lo = _split_bf16(sq[:, c * 256:(c + 1) * 256])
        outs.append(_dot(hi, gmat) + _dot(lo, gmat))
    return jnp.concatenate(outs, axis=1)


def _proj_kernel(n_prompt_tiles, xp_ref, xs_ref, g_ref, w_ref, o_ref):
    is_prompt = pl.program_id(1) < n_prompt_tiles
    x = jnp.where(is_prompt, xp_ref[...], xs_ref[...])
    ms = jnp.mean(x * x, axis=-1, keepdims=True)
    h = ((x * lax.rsqrt(ms + RMS_EPS)) * g_ref[...]).astype(BF16)
    o_ref[...] = _dot(h, w_ref[...])


def _proj(x_p, x_s, g, w_bf16, tm, tn):
    d = x_p.shape[1]
    n = x_p.shape[0] + x_s.shape[0]
    npt = x_p.shape[0] // tm
    cols = w_bf16.shape[1]
    return pl.pallas_call(
        functools.partial(_proj_kernel, npt),
        grid=(cols // tn, n // tm),
        in_specs=[pl.BlockSpec((tm, d), lambda j, i: (jnp.minimum(i, npt - 1), 0)),
                  pl.BlockSpec((tm, d), lambda j, i: (jnp.maximum(i - npt, 0), 0)),
                  pl.BlockSpec((1, d), lambda j, i: (0, 0)),
                  pl.BlockSpec((d, tn), lambda j, i: (0, j))],
        out_specs=pl.BlockSpec((tm, tn), lambda j, i: (i, j)),
        out_shape=jax.ShapeDtypeStruct((n, cols), F32),
        compiler_params=_cparams(("arbitrary", "arbitrary")),
        name="proj",
    )(x_p, x_s, g, w_bf16)


def _hgrn_chunk(q_raw, f_raw, v, g_raw, lb, ng, states):
    q = q_raw * _sigmoid(q_raw)
    f = lb + (1.0 - lb) * _sigmoid(f_raw)
    logf = jnp.log(f)
    k = 1.0 - f
    gate = _sigmoid(g_raw)

    row = lax.broadcasted_iota(I32, (HG_TILE, HG_TILE), 0)
    col = lax.broadcasted_iota(I32, (HG_TILE, HG_TILE), 1)
    causal = col <= row
    tri = causal.astype(BF16)
    hi, lo = _split_bf16(logf)
    b = _dot(tri, hi) + _dot(tri, lo)
    b_last = b[HG_TILE - 1:HG_TILE, :]
    q_dec = q * jnp.exp(b)
    k_dec = k * jnp.exp(b_last - b)

    outs, new_states = [], []
    for h in range(HG_HEADS):
        sl = slice(h * HG_DIM, (h + 1) * HG_DIM)
        st = states[h]
        bh = b[:, sl]
        qh = q[:, sl]
        kh = k[:, sl]
        vh = v[:, sl].astype(BF16)
        o = _dot_nt(q_dec[:, sl].astype(BF16), st.astype(BF16))
        rows = []
        for i in range(HG_TILE // HG_SUB):
            r0 = i * HG_SUB
            base = bh[r0 - 1:r0, :] if i else jnp.zeros((1, HG_DIM), F32)
            qi = qh[r0:r0 + HG_SUB, :] * jnp.exp(bh[r0:r0 + HG_SUB, :] - base)
            ki = kh * jnp.exp(jnp.minimum(base - bh, EXP_CLAMP))
            rows.append(_dot_nt(qi.astype(BF16), ki.astype(BF16)))
        scores = jnp.where(causal, jnp.concatenate(rows, axis=0), 0.0)
        o = o + _dot(scores.astype(BF16), vh)
        new_states.append(st * jnp.exp(b_last[:, sl])
                          + _dot(v[:, sl].T.astype(BF16), k_dec[:, sl].astype(BF16)))
        ms = jnp.mean(o * o, axis=-1, keepdims=True)
        outs.append((o * lax.rsqrt(ms + RMS_EPS)) * ng)
    return jnp.concatenate(outs, axis=1) * gate, new_states


def _hgrn_kernel(n_prompt_steps, group, q_ref, f_ref, i_ref, g_ref, lb_ref, ng_ref, s0_ref,
                 o_ref, sout_ref, st_scr):
    t = pl.program_id(0)
    lb = lb_ref[...]
    ng = ng_ref[...]

    def chunk(g, states):
        rows = slice(g * HG_TILE, (g + 1) * HG_TILE)
        out, states = _hgrn_chunk(q_ref[rows, :], f_ref[rows, :], i_ref[rows, :], g_ref[rows, :],
                                  lb, ng, states)
        o_ref[rows, :] = out.astype(o_ref.dtype)
        return states

    @pl.when(t == 0)
    def _():
        st_scr[...] = jnp.zeros(st_scr.shape, F32)

    @pl.when(t < n_prompt_steps)
    def _():
        states = [st_scr[h] for h in range(HG_HEADS)]
        for g in range(group):
            states = chunk(g, states)
        for h in range(HG_HEADS):
            st_scr[h] = states[h]

    @pl.when(t == n_prompt_steps - 1)
    def _():
        for h in range(HG_HEADS):
            sout_ref[0, h] = st_scr[h].T
        if group > 1:
            sout_ref[1:group] = jnp.zeros((group - 1,) + tuple(sout_ref.shape[1:]), F32)

    @pl.when(t >= n_prompt_steps)
    def _():
        for g in range(group):
            states = chunk(g, [s0_ref[g, h].T for h in range(HG_HEADS)])
            for h in range(HG_HEADS):
                sout_ref[g, h] = states[h].T


def _hgrn(proj, lb, norm_g, state_s, n_prompt, group):
    n = proj.shape[0]
    rows = group * HG_TILE
    n_prompt_steps = n_prompt // rows
    s0_all = jnp.concatenate([jnp.zeros((group,) + state_s.shape[1:], F32), state_s], axis=0)
    seq = lambda t: (jnp.maximum(t - (n_prompt_steps - 1), 0), 0, 0, 0)
    col_spec = lambda c: pl.BlockSpec((rows, HEAD_WIDTH), lambda t, c=c: (t, c))
    return pl.pallas_call(
        functools.partial(_hgrn_kernel, n_prompt_steps, group),
        grid=(n // rows,),
        in_specs=[col_spec(0), col_spec(1), col_spec(2), col_spec(3),
                  pl.BlockSpec((1, HEAD_WIDTH), lambda t: (0, 0)),
                  pl.BlockSpec((1, HG_DIM), lambda t: (0, 0)),
                  pl.BlockSpec((group, HG_HEADS, HG_DIM, HG_DIM), seq)],
        out_specs=[pl.BlockSpec((rows, HEAD_WIDTH), lambda t: (t, 0)),
                   pl.BlockSpec((group, HG_HEADS, HG_DIM, HG_DIM), seq)],
        out_shape=[jax.ShapeDtypeStruct((n, HEAD_WIDTH), BF16),
                   jax.ShapeDtypeStruct(s0_all.shape, F32)],
        scratch_shapes=[pltpu.VMEM((HG_HEADS, HG_DIM, HG_DIM), F32)],
        compiler_params=_cparams(("arbitrary",)),
        name="hgrn",
    )(proj, proj, proj, proj, lb, norm_g, s0_all)


def _da_prep_kernel(n_prompt_tiles, dq_ref, dk_ref, dv_ref, gq_ref, gk_ref, gmat_ref,
                    qlo_ref, qhi_ref, qlot_ref, qhit_ref, kb_ref, vt_ref, kp_ref, ks_ref, vp_ref,
                    vs_ref):
    i = pl.program_id(0)
    gmat = gmat_ref[...]
    inv = 1.0 / DA_QK_DIM

    dq = dq_ref[...]
    qn = dq * lax.rsqrt(_group_sum(dq * dq, gmat) * inv + RMS_EPS) * gq_ref[...]
    qn = qn * (DA_QK_DIM ** -0.5 * LOG2_E)
    lane = lax.broadcasted_iota(I32, qn.shape, 1)
    first = (lane & DA_QK_DIM) == 0
    qlo_ref[...] = jnp.where(first, qn, 0.0).astype(BF16)
    qhi_ref[...] = jnp.where(first, 0.0, qn).astype(BF16)
    qt = qn.T
    first_t = (lax.broadcasted_iota(I32, qt.shape, 0) & DA_QK_DIM) == 0
    qlot_ref[...] = jnp.where(first_t, qt, 0.0).astype(BF16)
    qhit_ref[...] = jnp.where(first_t, 0.0, qt).astype(BF16)

    dk = dk_ref[...]
    kn = dk * lax.rsqrt(_group_sum(dk * dk, gmat) * inv + RMS_EPS) * gk_ref[...]
    kb_ref[...] = kn.astype(BF16)
    v = dv_ref[...]
    vt = v.T
    tail = (lax.broadcasted_iota(I32, (V_ROWS - DA_V_DIM, vt.shape[1]), 0) == 0).astype(F32)
    pieces = []
    for h in range(DA_HEADS):
        pieces += [vt[h * DA_V_DIM:(h + 1) * DA_V_DIM, :], tail]
    vt_ref[...] = jnp.concatenate(pieces, axis=0).astype(BF16)

    @pl.when(i < n_prompt_tiles)
    def _():
        kp_ref[...] = kn
        vp_ref[...] = v

    @pl.when(i >= n_prompt_tiles)
    def _():
        ks_ref[...] = kn
        vs_ref[...] = v


def _da_prep(proj, gq, gk, gmat, n_prompt, tm):
    n = proj.shape[0]
    npt = n_prompt // tm
    full = lambda: pl.BlockSpec((tm, HEAD_WIDTH), lambda i: (i, 0))
    p_spec = lambda: pl.BlockSpec((tm, HEAD_WIDTH), lambda i: (jnp.minimum(i, npt - 1), 0))
    s_spec = lambda: pl.BlockSpec((tm, HEAD_WIDTH), lambda i: (jnp.maximum(i - npt, 0), 0))
    vec = lambda: pl.BlockSpec((1, HEAD_WIDTH), lambda i: (0, 0))
    return pl.pallas_call(
        functools.partial(_da_prep_kernel, npt),
        grid=(n // tm,),
        in_specs=[pl.BlockSpec((tm, HEAD_WIDTH), lambda i: (i, 4)),
                  pl.BlockSpec((tm, HEAD_WIDTH), lambda i: (i, 5)),
                  pl.BlockSpec((tm, HEAD_WIDTH), lambda i: (i, 6)),
                  vec(), vec(),
                  pl.BlockSpec((256, 256), lambda i: (0, 0))],
        out_specs=[full(), full(),
                   pl.BlockSpec((HEAD_WIDTH, tm), lambda i: (0, i)),
                   pl.BlockSpec((HEAD_WIDTH, tm), lambda i: (0, i)),
                   full(),
                   pl.BlockSpec((DA_HEADS * V_ROWS, tm), lambda i: (0, i)),
                   p_spec(), s_spec(), p_spec(), s_spec()],
        out_shape=[jax.ShapeDtypeStruct((n, HEAD_WIDTH), BF16),
                   jax.ShapeDtypeStruct((n, HEAD_WIDTH), BF16),
                   jax.ShapeDtypeStruct((HEAD_WIDTH, n), BF16),
                   jax.ShapeDtypeStruct((HEAD_WIDTH, n), BF16),
                   jax.ShapeDtypeStruct((n, HEAD_WIDTH), BF16),
                   jax.ShapeDtypeStruct((DA_HEADS * V_ROWS, n), BF16),
                   jax.ShapeDtypeStruct((n_prompt, HEAD_WIDTH), F32),
                   jax.ShapeDtypeStruct((n - n_prompt, HEAD_WIDTH), F32),
                   jax.ShapeDtypeStruct((n_prompt, HEAD_WIDTH), F32),
                   jax.ShapeDtypeStruct((n - n_prompt, HEAD_WIDTH), F32)],
        compiler_params=_cparams(("arbitrary",)),
        name="da_prep",
    )(proj, proj, proj, gq, gk, gmat)


def _lambda_of(lp, lam_init):
    a = jnp.sum(lp[0:1, :] * lp[1:2, :], axis=-1, keepdims=True)
    b = jnp.sum(lp[2:3, :] * lp[3:4, :], axis=-1, keepdims=True)
    return jnp.exp(a) - jnp.exp(b) + lam_init


def _subln(o, g, lam_init):
    ms = jnp.mean(o * o, axis=-1, keepdims=True)
    return (o * lax.rsqrt(ms + RMS_EPS)) * g * (1.0 - lam_init)


def _attn_p_kernel(lam_init, tile, qi_ref, kj_ref, bounded_ref, qlot_ref, qhit_ref, k_ref, vt_ref,
                   bound_ref, lp_ref, sg_ref, o_ref, m_scr, acc_scr):
    s = pl.program_id(0)
    qi = qi_ref[s]
    kj = kj_ref[s]
    bounded = bounded_ref[0] != 0
    n_maps = 2 * DA_HEADS

    @pl.when(kj == 0)
    def _():
        m_scr[...] = jnp.full(m_scr.shape, -jnp.inf, F32)
        acc_scr[...] = jnp.zeros(acc_scr.shape, F32)

    def scores(r, visible):
        h, c = divmod(r, 2)
        sl = slice(h * DA_V_DIM, (h + 1) * DA_V_DIM)
        sc = _dot(k_ref[:, sl], (qlot_ref, qhit_ref)[c][sl, :])
        return sc if visible is None else jnp.where(visible, sc, -jnp.inf)

    def sweep(visible):
        sc = scores(0, visible)
        for r in range(n_maps):
            nxt = scores(r + 1, visible) if r + 1 < n_maps else None
            h = r // 2
            m_prev = m_scr[r:r + 1, :]
            m_new = jnp.maximum(m_prev, jnp.max(sc, axis=0, keepdims=True))
            alpha = jnp.exp2(m_prev - m_new)
            e = jnp.exp2(sc - m_new).astype(BF16)
            acc_scr[r] = alpha * acc_scr[r] + _dot(vt_ref[h * V_ROWS:(h + 1) * V_ROWS, :], e)
            m_scr[r:r + 1, :] = m_new
            sc = nxt

    def sweep_bounded(visible):
        bound = bound_ref[0:1, 0:1]
        for r in range(n_maps):
            h = r // 2
            e = jnp.exp2(scores(r, None) - bound)
            if visible is not None:
                e = jnp.where(visible, e, 0.0)
            acc_scr[r] += _dot(vt_ref[h * V_ROWS:(h + 1) * V_ROWS, :], e.astype(BF16))

    def chunk_mask():
        shift = CHUNK.bit_length() - 1
        key_chunk = lax.broadcasted_iota(I32, (tile, tile), 0) >> shift
        qry_chunk = lax.broadcasted_iota(I32, (tile, tile), 1) >> shift
        return key_chunk <= qry_chunk

    off_diag = kj < qi
    on_diag = kj == qi
    unbounded = jnp.logical_not(bounded)

    @pl.when(jnp.logical_and(off_diag, bounded))
    def _():
        sweep_bounded(None)

    @pl.when(jnp.logical_and(on_diag, bounded))
    def _():
        sweep_bounded(chunk_mask())

    @pl.when(jnp.logical_and(off_diag, unbounded))
    def _():
        sweep(None)

    @pl.when(jnp.logical_and(on_diag, unbounded))
    def _():
        sweep(chunk_mask())

    @pl.when(on_diag)
    def _():
        lam = _lambda_of(lp_ref[...], lam_init)
        sg = sg_ref[...]
        outs = []
        for h in range(DA_HEADS):
            acc0 = acc_scr[2 * h]
            acc1 = acc_scr[2 * h + 1]
            a0 = acc0[0:DA_V_DIM, :] / acc0[DA_V_DIM:DA_V_DIM + 1, :]
            a1 = acc1[0:DA_V_DIM, :] / acc1[DA_V_DIM:DA_V_DIM + 1, :]
            outs.append(_subln((a0 - lam * a1).T, sg, lam_init))
        o_ref[...] = jnp.concatenate(outs, axis=1).astype(o_ref.dtype)


def _attn_prompt(qlot, qhit, kb, vt, bound, lp, sg, n_prompt, lam_init, tile):
    nq = n_prompt // tile
    qi_l, kj_l = [], []
    for i in range(nq):
        for j in range(i + 1):
            qi_l.append(i)
            kj_l.append(j)
    qi = jnp.asarray(qi_l, I32)
    kj = jnp.asarray(kj_l, I32)
    bounded = (bound <= ATTN_SAFE_BOUND).astype(I32).reshape(1)
    bound_row = jnp.broadcast_to(bound.astype(F32), (1, LANES))
    const = lambda shp: pl.BlockSpec(shp, lambda s, qi, kj, bd: (0, 0))
    grid_spec = pltpu.PrefetchScalarGridSpec(
        num_scalar_prefetch=3,
        grid=(len(qi_l),),
        in_specs=[pl.BlockSpec((HEAD_WIDTH, tile), lambda s, qi, kj, bd: (0, qi[s])),
                  pl.BlockSpec((HEAD_WIDTH, tile), lambda s, qi, kj, bd: (0, qi[s])),
                  pl.BlockSpec((tile, HEAD_WIDTH), lambda s, qi, kj, bd: (kj[s], 0)),
                  pl.BlockSpec((DA_HEADS * V_ROWS, tile), lambda s, qi, kj, bd: (0, kj[s])),
                  const((1, LANES)), const((4, DA_QK_DIM)), const((1, DA_V_DIM))],
        out_specs=pl.BlockSpec((tile, HEAD_WIDTH), lambda s, qi, kj, bd: (qi[s], 0)),
        scratch_shapes=[pltpu.VMEM((2 * DA_HEADS, tile), F32),
                        pltpu.VMEM((2 * DA_HEADS, V_ROWS, tile), F32)])
    return pl.pallas_call(
        functools.partial(_attn_p_kernel, lam_init, tile),
        grid_spec=grid_spec,
        out_shape=jax.ShapeDtypeStruct((n_prompt, HEAD_WIDTH), BF16),
        compiler_params=_cparams(("arbitrary",)),
        name="attn_p",
    )(qi, kj, bounded, qlot, qhit, kb, vt, bound_row, lp, sg)


def _attn_s_kernel(lam_init, qlo_ref, qhi_ref, kn_ref, vn_ref, ckt_ref, cv_ref, lp_ref, sg_ref,
                   o_ref):
    past = ckt_ref.shape[-1]
    lam = _lambda_of(lp_ref[...], lam_init)
    sg = sg_ref[...]
    outs = []
    for h in range(DA_HEADS):
        sl = slice(h * DA_V_DIM, (h + 1) * DA_V_DIM)
        kn = kn_ref[:, sl].astype(BF16)
        vc = cv_ref[0, pl.ds(h, past, stride=DA_HEADS), :].astype(BF16)
        vn = vn_ref[:, sl].astype(BF16)
        acc = None
        for c, q_ref in enumerate((qlo_ref, qhi_ref)):
            qh = q_ref[:, sl]
            kct = ckt_ref[0, h, c].astype(BF16)
            s_c = _dot(qh[:, c * DA_QK_DIM:(c + 1) * DA_QK_DIM], kct)
            s_n = _dot_nt(qh, kn)
            m = jnp.maximum(jnp.max(s_c, axis=-1, keepdims=True),
                            jnp.max(s_n, axis=-1, keepdims=True))
            e_c = jnp.exp2(s_c - m)
            e_n = jnp.exp2(s_n - m)
            l = jnp.sum(e_c, axis=-1, keepdims=True) + jnp.sum(e_n, axis=-1, keepdims=True)
            part = (_dot(e_c.astype(BF16), vc) + _dot(e_n.astype(BF16), vn)) / l
            acc = part if c == 0 else acc - lam * part
        outs.append(_subln(acc, sg, lam_init))
    o_ref[...] = jnp.concatenate(outs, axis=1).astype(o_ref.dtype)


def _attn_sample(qlo, qhi, ks, vs, cache_k, cache_v, lp, sg, n_prompt, lam_init):
    bs, past = cache_k.shape[0], cache_k.shape[1]
    cache_kt = jnp.transpose(cache_k, (0, 2, 3, 4, 1))
    cache_v = cache_v.reshape(bs, past * DA_HEADS, DA_V_DIM)
    t = ks.shape[0] // bs
    off = n_prompt // t
    return pl.pallas_call(
        functools.partial(_attn_s_kernel, lam_init),
        grid=(bs,),
        in_specs=[pl.BlockSpec((t, HEAD_WIDTH), lambda b: (off + b, 0)),
                  pl.BlockSpec((t, HEAD_WIDTH), lambda b: (off + b, 0)),
                  pl.BlockSpec((t, HEAD_WIDTH), lambda b: (b, 0)),
                  pl.BlockSpec((t, HEAD_WIDTH), lambda b: (b, 0)),
                  pl.BlockSpec((1, DA_HEADS, 2, DA_QK_DIM, past), lambda b: (b, 0, 0, 0, 0)),
                  pl.BlockSpec((1, past * DA_HEADS, DA_V_DIM), lambda b: (b, 0, 0)),
                  pl.BlockSpec((4, DA_QK_DIM), lambda b: (0, 0)),
                  pl.BlockSpec((1, DA_V_DIM), lambda b: (0, 0))],
        out_specs=pl.BlockSpec((t, HEAD_WIDTH), lambda b: (b, 0)),
        out_shape=jax.ShapeDtypeStruct((bs * t, HEAD_WIDTH), BF16),
        compiler_params=_cparams(("arbitrary",)),
        name="attn_s",
    )(qlo, qhi, ks, vs, cache_kt, cache_v, lp, sg)


def _pack_bf16_pairs(x):
    half = x.shape[1] // 2
    lo = lax.bitcast_convert_type(x[:, :half].astype(BF16).astype(F32), U32)
    hi = lax.bitcast_convert_type(x[:, half:].astype(BF16).astype(F32), U32)
    return (lo >> 16) | (hi & jnp.uint32(0xFFFF0000))


def _unpack_bf16_pairs(w):
    lo = lax.bitcast_convert_type(w << 16, F32).astype(BF16)
    hi = lax.bitcast_convert_type(w & jnp.uint32(0xFFFF0000), F32).astype(BF16)
    return lo, hi


def _mix_kernel(n_prompt_tiles, hg_ref, dap_ref, das_ref, xp_ref, xs_ref, wo_ref, g2_ref, wr_ref,
                br_ref, x1_ref, h2p_ref, route_ref, gates_ref, cnt_ref, base_scr):
    i = pl.program_id(0)

    @pl.when(i == 0)
    def _():
        base_scr[...] = jnp.zeros(base_scr.shape, F32)

    tm = xp_ref.shape[0]
    is_prompt = i < n_prompt_tiles
    da = jnp.where(is_prompt, dap_ref[...], das_ref[...])
    x = jnp.where(is_prompt, xp_ref[...], xs_ref[...])
    x1 = x + _dot(hg_ref[...], wo_ref[0:HEAD_WIDTH, :]) \
        + _dot(da, wo_ref[HEAD_WIDTH:2 * HEAD_WIDTH, :])
    x1_ref[...] = x1
    ms = jnp.mean(x1 * x1, axis=-1, keepdims=True)
    h2 = (x1 * lax.rsqrt(ms + RMS_EPS)) * g2_ref[...]
    h2p_ref[...] = _pack_bf16_pairs(h2)

    h_hi, h_lo = _split_bf16(h2)
    w_hi, w_lo = _split_bf16(wr_ref[...])
    logits = _dot(h_hi, w_hi) + _dot(h_lo, w_hi) + _dot(h_hi, w_lo) + br_ref[...]
    lane = lax.broadcasted_iota(I32, logits.shape, 1)
    lanef = lane.astype(F32)
    lg = jnp.where(lane < N_EXPERTS, logits, -jnp.inf)

    sels, vals, idxs = [], [], []
    for _ in range(TOP_K):
        mx = jnp.max(lg, axis=-1, keepdims=True)
        idx = jnp.min(jnp.where(lg == mx, lanef, float(LANES)), axis=-1, keepdims=True)
        sel = lanef == idx
        lg = jnp.where(sel, -jnp.inf, lg)
        sels.append(sel)
        vals.append(mx)
        idxs.append(idx)
    es = [jnp.exp(v - vals[0]) for v in vals]
    den = es[0] + es[1] + es[2] + es[3]

    onehot = jnp.zeros(logits.shape, F32)
    for sel in sels:
        onehot = onehot + sel.astype(F32)
    row = lax.broadcasted_iota(I32, (tm, tm), 0)
    col = lax.broadcasted_iota(I32, (tm, tm), 1)
    strict = (col < row).astype(BF16)
    before = _dot(strict, onehot.astype(BF16)) + base_scr[0:1, :]
    base_scr[0:1, :] = base_scr[0:1, :] + jnp.sum(onehot, axis=0, keepdims=True)

    route = jnp.zeros(logits.shape, F32)
    gates = jnp.zeros(logits.shape, F32)
    for r in range(TOP_K):
        rank = jnp.sum(jnp.where(sels[r], before, 0.0), axis=-1, keepdims=True)
        route = jnp.where(lane == r, idxs[r], route)
        route = jnp.where(lane == TOP_K + r, rank, route)
        gates = jnp.where(lane == r, es[r] / den, gates)
    route_ref[...] = route.astype(I32)
    gates_ref[...] = gates
    cnt_ref[...] = jnp.broadcast_to(base_scr[0:1, :], cnt_ref.shape).astype(I32)


def _mix(hg, da_p, da_s, x_p, x_s, wo_bf16, g2, wr_pad, br_pad, tm):
    d = x_p.shape[1]
    n = x_p.shape[0] + x_s.shape[0]
    npt = x_p.shape[0] // tm
    row = lambda w: pl.BlockSpec((tm, w), lambda i: (i, 0))
    p_row = lambda w: pl.BlockSpec((tm, w), lambda i: (jnp.minimum(i, npt - 1), 0))
    s_row = lambda w: pl.BlockSpec((tm, w), lambda i: (jnp.maximum(i - npt, 0), 0))
    const = lambda shp: pl.BlockSpec(shp, lambda i: (0, 0))
    return pl.pallas_call(
        functools.partial(_mix_kernel, npt),
        grid=(n // tm,),
        in_specs=[row(HEAD_WIDTH), p_row(HEAD_WIDTH), s_row(HEAD_WIDTH), p_row(d), s_row(d),
                  const((2 * HEAD_WIDTH, d)), const((1, d)), const((d, LANES)), const((1, LANES))],
        out_specs=[row(d), row(d // 2), row(LANES), row(LANES), const((8, LANES))],
        out_shape=[jax.ShapeDtypeStruct((n, d), F32),
                   jax.ShapeDtypeStruct((n, d // 2), U32),
                   jax.ShapeDtypeStruct((n, LANES), I32),
                   jax.ShapeDtypeStruct((n, LANES), F32),
                   jax.ShapeDtypeStruct((8, LANES), I32)],
        scratch_shapes=[pltpu.VMEM((8, LANES), F32)],
        compiler_params=_cparams(("arbitrary",)),
        name="mix",
    )(hg, da_p, da_s, x_p, x_s, wo_bf16, g2, wr_pad, br_pad)


def _dispatch_copy(h_ref, xs_ref, sem, dest_ref, r, kk):
    return pltpu.make_async_copy(h_ref.at[pl.ds(r, 1)],
                                 xs_ref.at[pl.ds(dest_ref[r * TOP_K + kk], 1)], sem)


def _dispatch_kernel(zinfo_ref, dest_ref, h_ref, xs_ref, zbuf, sem, zsem):
    tm = h_ref.shape[0]
    bm = zbuf.shape[0]
    nb = xs_ref.shape[0] // bm

    def zero_copy(start):
        return pltpu.make_async_copy(zbuf, xs_ref.at[pl.ds(pl.multiple_of(start, bm), bm)], zsem)

    def for_each_zero_block(fn):
        for e in range(N_EXPERTS):
            start = zinfo_ref[e]

            @pl.when(start >= 0)
            def _():
                fn(zero_copy(start))

        def tail(b, carry):
            fn(zero_copy(b * bm))
            return carry

        lax.fori_loop(zinfo_ref[N_EXPERTS], nb, tail, 0)

    @pl.when(pl.program_id(0) == 0)
    def _():
        zbuf[...] = jnp.zeros(zbuf.shape, zbuf.dtype)
        for_each_zero_block(lambda cp: cp.start())
        for_each_zero_block(lambda cp: cp.wait())

    def issue(r, carry):
        for kk in range(TOP_K):
            _dispatch_copy(h_ref, xs_ref, sem, dest_ref, r, kk).start()
        return carry

    def drain(r, carry):
        for kk in range(TOP_K):
            _dispatch_copy(h_ref, xs_ref, sem, dest_ref, r, kk).wait()
        return carry

    lax.fori_loop(0, tm, issue, 0)
    lax.fori_loop(0, tm, drain, 0)


def _dispatch(h2p, dest_flat, zinfo, n_rows, bm, tm):
    n, w = h2p.shape
    grid_spec = pltpu.PrefetchScalarGridSpec(
        num_scalar_prefetch=1,
        grid=(n // tm,),
        in_specs=[pl.BlockSpec((tm * TOP_K,), lambda i, z: (i,), memory_space=pltpu.SMEM),
                  pl.BlockSpec((tm, w), lambda i, z: (i, 0))],
        out_specs=pl.BlockSpec(memory_space=pl.ANY),
        scratch_shapes=[pltpu.VMEM((bm, w), h2p.dtype),
                        pltpu.SemaphoreType.DMA, pltpu.SemaphoreType.DMA])
    return pl.pallas_call(
        _dispatch_kernel,
        grid_spec=grid_spec,
        out_shape=jax.ShapeDtypeStruct((n_rows, w), h2p.dtype),
        compiler_params=_cparams(("arbitrary",)),
        name="dispatch",
    )(zinfo, dest_flat, h2p)


def _moe_kernel(be_ref, nu_ref, xs_ref, wg_ref, wu_ref, bg_ref, bu_ref, wd_ref, bd_ref,
                o_ref, xb_scr):
    i = pl.program_id(0)
    j = pl.program_id(1)
    used = i < nu_ref[0]

    @pl.when(jnp.logical_and(j == 0, used))
    def _():
        half = xs_ref.shape[1]
        lo, hi = _unpack_bf16_pairs(xs_ref[...])
        xb_scr[:, 0:half] = lo
        xb_scr[:, half:2 * half] = hi
        o_ref[...] = jnp.broadcast_to(bd_ref[0], o_ref.shape)

    @pl.when(jnp.logical_and(j == 0, jnp.logical_not(used)))
    def _():
        o_ref[...] = jnp.zeros(o_ref.shape, F32)

    @pl.when(used)
    def _():
        xb = xb_scr[...]
        gate = _dot(xb, wg_ref[0].astype(BF16)) + bg_ref[0]
        up = _dot(xb, wu_ref[0].astype(BF16)) + bu_ref[0]
        gate = jnp.minimum(gate, SWIGLU_LIMIT)
        up = jnp.clip(up, -SWIGLU_LIMIT, SWIGLU_LIMIT)
        glu = gate * _sigmoid(SWIGLU_ALPHA * gate)
        mid = ((up + 1.0) * glu).astype(BF16)
        o_ref[...] += _dot(mid, wd_ref[0].astype(BF16))


def _moe(xs, block_expert, n_used, w_gu, b_gu, w_down, b_down, bm, th):
    rows, half = xs.shape
    d = 2 * half
    n_e, _, two_h = w_gu.shape
    hid = two_h // 2
    nj = hid // th
    nb = rows // bm
    b_gu3 = b_gu.reshape(n_e, 1, two_h)
    b_dn3 = b_down.reshape(n_e, 1, d)

    def xrow(i, j, be, nu):
        return (jnp.minimum(i, jnp.maximum(nu[0] - 1, 0)), 0)

    def hcol(i, j, nu):
        return jnp.where(i < nu[0], j, nj - 1)

    grid_spec = pltpu.PrefetchScalarGridSpec(
        num_scalar_prefetch=2,
        grid=(nb, nj),
        in_specs=[pl.BlockSpec((bm, half), xrow),
                  pl.BlockSpec((1, d, th), lambda i, j, be, nu: (be[i], 0, hcol(i, j, nu))),
                  pl.BlockSpec((1, d, th), lambda i, j, be, nu: (be[i], 0, nj + hcol(i, j, nu))),
                  pl.BlockSpec((1, 1, th), lambda i, j, be, nu: (be[i], 0, hcol(i, j, nu))),
                  pl.BlockSpec((1, 1, th), lambda i, j, be, nu: (be[i], 0, nj + hcol(i, j, nu))),
                  pl.BlockSpec((1, th, d), lambda i, j, be, nu: (be[i], hcol(i, j, nu), 0)),
                  pl.BlockSpec((1, 1, d), lambda i, j, be, nu: (be[i], 0, 0))],
        out_specs=pl.BlockSpec((bm, d), lambda i, j, be, nu: (i, 0)),
        scratch_shapes=[pltpu.VMEM((bm, d), BF16)])
    return pl.pallas_call(
        _moe_kernel,
        grid_spec=grid_spec,
        out_shape=jax.ShapeDtypeStruct((rows, d), F32),
        compiler_params=_cparams(("arbitrary", "arbitrary")),
        name="moe",
    )(block_expert, n_used, xs, w_gu, w_gu, b_gu3, b_gu3, w_down, b_dn3)


def _combine_kernel(n_prompt_tiles, n_tiles, dest_ref, dest_next_ref, x1_ref, gates_ref, yb_ref,
                    yp_ref, ys_ref, buf, sem):
    i = pl.program_id(0)
    tm = x1_ref.shape[0]
    slot = i % 2

    def for_each_row_copy(d_ref, s, fn):
        def body(r, carry):
            for kk in range(TOP_K):
                fn(pltpu.make_async_copy(yb_ref.at[pl.ds(d_ref[r * TOP_K + kk], 1)],
                                         buf.at[s, kk, pl.ds(r, 1)], sem.at[s]))
            return carry

        lax.fori_loop(0, tm, body, 0)

    @pl.when(i == 0)
    def _():
        for_each_row_copy(dest_ref, 0, lambda cp: cp.start())

    @pl.when(i + 1 < n_tiles)
    def _():
        for_each_row_copy(dest_next_ref, 1 - slot, lambda cp: cp.start())

    for_each_row_copy(dest_ref, slot, lambda cp: cp.wait())

    g = gates_ref[...]
    y = x1_ref[...]
    for kk in range(TOP_K):
        y = y + g[:, kk:kk + 1] * buf[slot, kk]

    @pl.when(i < n_prompt_tiles)
    def _():
        yp_ref[...] = y

    @pl.when(i >= n_prompt_tiles)
    def _():
        ys_ref[...] = y


def _combine(x1, gates, dest_flat, yb, n_prompt, tm):
    n, d = x1.shape
    npt = n_prompt // tm
    n_tiles = n // tm
    return pl.pallas_call(
        functools.partial(_combine_kernel, npt, n_tiles),
        grid=(n_tiles,),
        in_specs=[pl.BlockSpec((tm * TOP_K,), lambda i: (i,), memory_space=pltpu.SMEM),
                  pl.BlockSpec((tm * TOP_K,), lambda i: (jnp.minimum(i + 1, n_tiles - 1),),
                               memory_space=pltpu.SMEM),
                  pl.BlockSpec((tm, d), lambda i: (i, 0)),
                  pl.BlockSpec((tm, LANES), lambda i: (i, 0)),
                  pl.BlockSpec(memory_space=pl.ANY)],
        out_specs=[pl.BlockSpec((tm, d), lambda i: (jnp.minimum(i, npt - 1), 0)),
                   pl.BlockSpec((tm, d), lambda i: (jnp.maximum(i - npt, 0), 0))],
        out_shape=[jax.ShapeDtypeStruct((n_prompt, d), F32),
                   jax.ShapeDtypeStruct((n - n_prompt, d), F32)],
        scratch_shapes=[pltpu.VMEM((2, TOP_K, tm, d), F32), pltpu.SemaphoreType.DMA((2,))],
        compiler_params=_cparams(("arbitrary",)),
        name="combine",
    )(dest_flat, dest_flat, x1, gates, yb)


def _group_matrix(group):
    r = jnp.arange(256) // group
    return (r[:, None] == r[None, :]).astype(BF16)


def _tile(n, want):
    t = min(want, n)
    while n % t:
        t //= 2
    return t


def _layer(x_prompt, x_sample, cache_k, cache_v, state_hgrn, attn_norm_g, w_in, lb, hg_norm_g,
           da_q_norm_g, da_k_norm_g, da_lambda, da_subln_g, w_out, ffn_norm_g, w_router,
           b_router, w_gu, b_gu, w_down, b_down, lam_init):
    bp, s, d = x_prompt.shape
    bs, t, _ = x_sample.shape
    assert bp == 1 and t == HG_TILE and s % HG_TILE == 0
    n_p = bp * s
    n = n_p + bs * t
    x_p = x_prompt.reshape(n_p, d)
    x_s = x_sample.reshape(bs * t, d)
    row_gcd = math.gcd(n_p, bs * t)

    proj = _proj(x_p, x_s, attn_norm_g.reshape(1, d), w_in.astype(BF16), _tile(row_gcd, PROJ_TM),
                 PROJ_TN)

    hg_group = _tile(math.gcd(n_p // HG_TILE, bs), HG_GROUP)
    hg, s_all = _hgrn(proj, lb.reshape(1, HEAD_WIDTH), hg_norm_g.reshape(1, HG_DIM), state_hgrn,
                      n_p, hg_group)

    tm_da = _tile(row_gcd, DA_TM)
    gq = jnp.tile(da_q_norm_g, 2 * DA_HEADS).reshape(1, HEAD_WIDTH)
    gk = jnp.tile(da_k_norm_g, 2 * DA_HEADS).reshape(1, HEAD_WIDTH)
    qlo, qhi, qlot, qhit, kb, vt, k_p, k_s, v_p, v_s = _da_prep(proj, gq, gk,
                                                                _group_matrix(DA_QK_DIM),
                                                    n_p, tm_da)
    sg = da_subln_g.reshape(1, DA_V_DIM)
    score_bound = (DA_QK_DIM ** 0.5 * LOG2_E) * jnp.max(jnp.abs(da_q_norm_g)) \
        * jnp.max(jnp.abs(da_k_norm_g))
    da_p = _attn_prompt(qlot, qhit, kb, vt, score_bound, da_lambda, sg, n_p, lam_init,
                        _tile(n_p, ATTN_TILE))
    past = cache_k.shape[1]
    da_s = _attn_sample(qlo, qhi, k_s, v_s, cache_k, cache_v, da_lambda, sg, n_p, lam_init)

    wr_pad = jnp.pad(w_router, ((0, 0), (0, LANES - N_EXPERTS)))
    br_pad = jnp.pad(b_router, (0, LANES - N_EXPERTS)).reshape(1, LANES)
    x1, h2p, route, gates, cnt = _mix(hg, da_p, da_s, x_p, x_s, w_out.astype(BF16),
                                      ffn_norm_g.reshape(1, d), wr_pad, br_pad,
                                      _tile(row_gcd, MIX_TM))

    bm = MOE_BM
    counts = cnt[0, :N_EXPERTS]
    padded = (counts + bm - 1) // bm * bm
    pend = jnp.cumsum(padded)
    pstart = pend - padded
    dest = (pstart[route[:, 0:TOP_K]] + route[:, TOP_K:2 * TOP_K]).reshape(-1).astype(I32)
    nb = (n * TOP_K + N_EXPERTS * (bm - 1)) // bm
    n_used = (pend[-1] // bm).astype(I32)
    blk = jnp.minimum(jnp.arange(nb, dtype=I32), jnp.maximum(n_used - 1, 0))
    block_expert = jnp.minimum(jnp.sum(pend[None, :] <= (blk * bm)[:, None], axis=1),
                               N_EXPERTS - 1).astype(I32)
    zinfo = jnp.concatenate([jnp.where(padded > 0, pend - bm, -1), n_used[None]]).astype(I32)

    xs = _dispatch(h2p, dest, zinfo, nb * bm, bm, _tile(n, DISPATCH_TM))
    yb = _moe(xs, block_expert, n_used.reshape(1), w_gu, b_gu, w_down, b_down, bm,
              min(MOE_TH, w_down.shape[1]))
    y_p, y_s = _combine(x1, gates, dest, yb, n_p, _tile(row_gcd, COMBINE_TM))

    return (y_p.reshape(bp, s, d), y_s.reshape(bs, t, d),
            k_p.reshape(bp, s, DA_HEADS, 2, DA_QK_DIM), v_p.reshape(bp, s, DA_HEADS, DA_V_DIM),
            s_all[0:1],
            k_s.reshape(bs, t, DA_HEADS, 2, DA_QK_DIM), v_s.reshape(bs, t, DA_HEADS, DA_V_DIM),
            s_all[hg_group:])


def kernel(x_prompt, x_sample, cache_k, cache_v, state_hgrn, attn_norm_g, w_in, hg_lb_logits,
           hg_norm_g, da_q_norm_g, da_k_norm_g, da_lambda, da_subln_g, w_out, ffn_norm_g,
           w_router, b_router, w_gu, b_gu, w_down, b_down):
    depth = w_in.shape[0]
    assert depth == 1, "single-layer step"
    lb_all = jnp.cumsum(jax.nn.softmax(hg_lb_logits.astype(F32), axis=0), axis=0)
    lam_init = 0.8 - 0.6 * math.exp(-0.3 * 0)
    outs = _layer(x_prompt, x_sample, cache_k[0], cache_v[0], state_hgrn[0], attn_norm_g[0],
                  w_in[0], lb_all[0], hg_norm_g[0], da_q_norm_g[0], da_k_norm_g[0], da_lambda[0],
                  da_subln_g[0], w_out[0], ffn_norm_g[0], w_router[0], b_router[0], w_gu[0],
                  b_gu[0], w_down[0], b_down[0], lam_init)
    y_p, y_s, k_p, v_p, s_p, k_s, v_s, s_s = outs
    return (y_p, y_s, k_p[None], v_p[None], s_p[None], k_s[None], v_s[None], s_s[None])
```

```python
import functools
import math

import jax
import jax.numpy as jnp
from jax import lax
from jax.experimental import pallas as pl
from jax.experimental.pallas import tpu as pltpu

F32 = jnp.float32
BF16 = jnp.bfloat16
I32 = jnp.int32
U32 = jnp.uint32

RMS_EPS = 1e-6
CHUNK = 64
HG_HEADS = 8
HG_DIM = 128
DA_HEADS = 8
DA_QK_DIM = 64
DA_V_DIM = 128
HEAD_WIDTH = 1024
N_EXPERTS = 32
TOP_K = 4
SWIGLU_LIMIT = 7.0
SWIGLU_ALPHA = 1.702
LANES = 128
HG_TILE = 64
HG_SUB = 16
HG_GROUP = 4
EXP_CLAMP = 80.0
LOG2_E = 1.4426950408889634
V_ROWS = DA_V_DIM + 16
ATTN_SAFE_BOUND = 50.0
VMEM_LIMIT = 56 * 1024 * 1024
PROJ_TM, PROJ_TN = 256, 3584
DA_TM = 256
ATTN_TILE = 512
MIX_TM = 256
DISPATCH_TM = 256
MOE_BM, MOE_TH = 1024, 256
MOE_PARTS = 4
COMBINE_TM = 128


def _cparams(sem, vmem=VMEM_LIMIT):
    return pltpu.CompilerParams(dimension_semantics=sem, vmem_limit_bytes=vmem)


def _dot(a, b):
    return jnp.dot(a, b, preferred_element_type=F32)


def _dot_nt(a, b):
    return lax.dot_general(a, b, (((1,), (1,)), ((), ())), preferred_element_type=F32)


def _split_bf16(x):
    hi = x.astype(BF16)
    lo = (x - hi.astype(F32)).astype(BF16)
    return hi, lo


def _sigmoid(x):
    return 1.0 / (1.0 + jnp.exp(-x))


def _group_sum(sq, gmat):
    outs = []
    for c in range(sq.shape[1] // 256):
        hi, lo = _split_bf16(sq[:, c * 256:(c + 1) * 256])
        outs.append(_dot(hi, gmat) + _dot(lo, gmat))
    return jnp.concatenate(outs, axis=1)


def _proj_kernel(n_prompt_tiles, xp_ref, xs_ref, g_ref, w_ref, o_ref):
    is_prompt = pl.program_id(1) < n_prompt_tiles
    x = jnp.where(is_prompt, xp_ref[...], xs_ref[...])
    ms = jnp.mean(x * x, axis=-1, keepdims=True)
    h = ((x * lax.rsqrt(ms + RMS_EPS)) * g_ref[...]).astype(BF16)
    o_ref[...] = _dot(h, w_ref[...])


def _proj(x_p, x_s, g, w_bf16, tm, tn):
    d = x_p.shape[1]
    n = x_p.shape[0] + x_s.shape[0]
    npt = x_p.shape[0] // tm
    cols = w_bf16.shape[1]
    return pl.pallas_call(
        functools.partial(_proj_kernel, npt),
        grid=(cols // tn, n // tm),
        in_specs=[pl.BlockSpec((tm, d), lambda j, i: (jnp.minimum(i, npt - 1), 0)),
                  pl.BlockSpec((tm, d), lambda j, i: (jnp.maximum(i - npt, 0), 0)),
                  pl.BlockSpec((1, d), lambda j, i: (0, 0)),
                  pl.BlockSpec((d, tn), lambda j, i: (0, j))],
        out_specs=pl.BlockSpec((tm, tn), lambda j, i: (i, j)),
        out_shape=jax.ShapeDtypeStruct((n, cols), F32),
        compiler_params=_cparams(("arbitrary", "arbitrary")),
        name="proj",
    )(x_p, x_s, g, w_bf16)


def _hgrn_chunk(q_raw, f_raw, v, g_raw, lb, ng, states):
    q = q_raw * _sigmoid(q_raw)
    f = lb + (1.0 - lb) * _sigmoid(f_raw)
    logf = jnp.log(f)
    k = 1.0 - f
    gate = _sigmoid(g_raw)

    row = lax.broadcasted_iota(I32, (HG_TILE, HG_TILE), 0)
    col = lax.broadcasted_iota(I32, (HG_TILE, HG_TILE), 1)
    causal = col <= row
    tri = causal.astype(BF16)
    hi, lo = _split_bf16(logf)
    b = _dot(tri, hi) + _dot(tri, lo)
    b_last = b[HG_TILE - 1:HG_TILE, :]
    q_dec = q * jnp.exp(b)
    k_dec = k * jnp.exp(b_last - b)

    outs, new_states = [], []
    for h in range(HG_HEADS):
        sl = slice(h * HG_DIM, (h + 1) * HG_DIM)
        st = states[h]
        bh = b[:, sl]
        qh = q[:, sl]
        kh = k[:, sl]
        vh = v[:, sl].astype(BF16)
        o = _dot_nt(q_dec[:, sl].astype(BF16), st.astype(BF16))
        rows = []
        for i in range(HG_TILE // HG_SUB):
            r0 = i * HG_SUB
            base = bh[r0 - 1:r0, :] if i else jnp.zeros((1, HG_DIM), F32)
            qi = qh[r0:r0 + HG_SUB, :] * jnp.exp(bh[r0:r0 + HG_SUB, :] - base)
            ki = kh * jnp.exp(jnp.minimum(base - bh, EXP_CLAMP))
            rows.append(_dot_nt(qi.astype(BF16), ki.astype(BF16)))
        scores = jnp.where(causal, jnp.concatenate(rows, axis=0), 0.0)
        o = o + _dot(scores.astype(BF16), vh)
        new_states.append(st * jnp.exp(b_last[:, sl])
                          + _dot(v[:, sl].T.astype(BF16), k_dec[:, sl].astype(BF16)))
        ms = jnp.mean(o * o, axis=-1, keepdims=True)
        outs.append((o * lax.rsqrt(ms + RMS_EPS)) * ng)
    return jnp.concatenate(outs, axis=1) * gate, new_states


def _hgrn_kernel(n_prompt_steps, group, q_ref, f_ref, i_ref, g_ref, lb_ref, ng_ref, s0_ref,
                 o_ref, sout_ref, st_scr):
    t = pl.program_id(0)
    lb = lb_ref[...]
    ng = ng_ref[...]

    def chunk(g, states):
        rows = slice(g * HG_TILE, (g + 1) * HG_TILE)
        out, states = _hgrn_chunk(q_ref[rows, :], f_ref[rows, :], i_ref[rows, :], g_ref[rows, :],
                                  lb, ng, states)
        o_ref[rows, :] = out.astype(o_ref.dtype)
        return states

    @pl.when(t == 0)
    def _():
        st_scr[...] = jnp.zeros(st_scr.shape, F32)

    @pl.when(t < n_prompt_steps)
    def _():
        states = [st_scr[h] for h in range(HG_HEADS)]
        for g in range(group):
            states = chunk(g, states)
        for h in range(HG_HEADS):
            st_scr[h] = states[h]

    @pl.when(t == n_prompt_steps - 1)
    def _():
        for h in range(HG_HEADS):
            sout_ref[0, h] = st_scr[h].T
        if group > 1:
            sout_ref[1:group] = jnp.zeros((group - 1,) + tuple(sout_ref.shape[1:]), F32)

    @pl.when(t >= n_prompt_steps)
    def _():
        for g in range(group):
            states = chunk(g, [s0_ref[g, h].T for h in range(HG_HEADS)])
            for h in range(HG_HEADS):
                sout_ref[g, h] = states[h].T


def _hgrn(proj, lb, norm_g, state_s, n_prompt, group):
    n = proj.shape[0]
    rows = group * HG_TILE
    n_prompt_steps = n_prompt // rows
    s0_all = jnp.concatenate([jnp.zeros((group,) + state_s.shape[1:], F32), state_s], axis=0)
    seq = lambda t: (jnp.maximum(t - (n_prompt_steps - 1), 0), 0, 0, 0)
    col_spec = lambda c: pl.BlockSpec((rows, HEAD_WIDTH), lambda t, c=c: (t, c))
    return pl.pallas_call(
        functools.partial(_hgrn_kernel, n_prompt_steps, group),
        grid=(n // rows,),
        in_specs=[col_spec(0), col_spec(1), col_spec(2), col_spec(3),
                  pl.BlockSpec((1, HEAD_WIDTH), lambda t: (0, 0)),
                  pl.BlockSpec((1, HG_DIM), lambda t: (0, 0)),
                  pl.BlockSpec((group, HG_HEADS, HG_DIM, HG_DIM), seq)],
        out_specs=[pl.BlockSpec((rows, HEAD_WIDTH), lambda t: (t, 0)),
                   pl.BlockSpec((group, HG_HEADS, HG_DIM, HG_DIM), seq)],
        out_shape=[jax.ShapeDtypeStruct((n, HEAD_WIDTH), BF16),
                   jax.ShapeDtypeStruct(s0_all.shape, F32)],
        scratch_shapes=[pltpu.VMEM((HG_HEADS, HG_DIM, HG_DIM), F32)],
        compiler_params=_cparams(("arbitrary",)),
        name="hgrn",
    )(proj, proj, proj, proj, lb, norm_g, s0_all)


def _da_prep_kernel(n_prompt_tiles, dq_ref, dk_ref, dv_ref, gq_ref, gk_ref, gmat_ref,
                    qlo_ref, qhi_ref, qlot_ref, qhit_ref, kb_ref, vt_ref, kp_ref, ks_ref, vp_ref,
                    vs_ref):
    i = pl.program_id(0)
    gmat = gmat_ref[...]
    inv = 1.0 / DA_QK_DIM

    dq = dq_ref[...]
    qn = dq * lax.rsqrt(_group_sum(dq * dq, gmat) * inv + RMS_EPS) * gq_ref[...]
    qn = qn * (DA_QK_DIM ** -0.5 * LOG2_E)
    lane = lax.broadcasted_iota(I32, qn.shape, 1)
    first = (lane & DA_QK_DIM) == 0
    qlo_ref[...] = jnp.where(first, qn, 0.0).astype(BF16)
    qhi_ref[...] = jnp.where(first, 0.0, qn).astype(BF16)
    qt = qn.T
    first_t = (lax.broadcasted_iota(I32, qt.shape, 0) & DA_QK_DIM) == 0
    qlot_ref[...] = jnp.where(first_t, qt, 0.0).astype(BF16)
    qhit_ref[...] = jnp.where(first_t, 0.0, qt).astype(BF16)

    dk = dk_ref[...]
    kn = dk * lax.rsqrt(_group_sum(dk * dk, gmat) * inv + RMS_EPS) * gk_ref[...]
    kb_ref[...] = kn.astype(BF16)
    v = dv_ref[...]
    vt = v.T
    tail = (lax.broadcasted_iota(I32, (V_ROWS - DA_V_DIM, vt.shape[1]), 0) == 0).astype(F32)
    pieces = []
    for h in range(DA_HEADS):
        pieces += [vt[h * DA_V_DIM:(h + 1) * DA_V_DIM, :], tail]
    vt_ref[...] = jnp.concatenate(pieces, axis=0).astype(BF16)

    @pl.when(i < n_prompt_tiles)
    def _():
        kp_ref[...] = kn
        vp_ref[...] = v

    @pl.when(i >= n_prompt_tiles)
    def _():
        ks_ref[...] = kn
        vs_ref[...] = v


def _da_prep(proj, gq, gk, gmat, n_prompt, tm):
    n = proj.shape[0]
    npt = n_prompt // tm
    full = lambda: pl.BlockSpec((tm, HEAD_WIDTH), lambda i: (i, 0))
    p_spec = lambda: pl.BlockSpec((tm, HEAD_WIDTH), lambda i: (jnp.minimum(i, npt - 1), 0))
    s_spec = lambda: pl.BlockSpec((tm, HEAD_WIDTH), lambda i: (jnp.maximum(i - npt, 0), 0))
    vec = lambda: pl.BlockSpec((1, HEAD_WIDTH), lambda i: (0, 0))
    return pl.pallas_call(
        functools.partial(_da_prep_kernel, npt),
        grid=(n // tm,),
        in_specs=[pl.BlockSpec((tm, HEAD_WIDTH), lambda i: (i, 4)),
                  pl.BlockSpec((tm, HEAD_WIDTH), lambda i: (i, 5)),
                  pl.BlockSpec((tm, HEAD_WIDTH), lambda i: (i, 6)),
                  vec(), vec(),
                  pl.BlockSpec((256, 256), lambda i: (0, 0))],
        out_specs=[full(), full(),
                   pl.BlockSpec((HEAD_WIDTH, tm), lambda i: (0, i)),
                   pl.BlockSpec((HEAD_WIDTH, tm), lambda i: (0, i)),
                   full(),
                   pl.BlockSpec((DA_HEADS * V_ROWS, tm), lambda i: (0, i)),
                   p_spec(), s_spec(), p_spec(), s_spec()],
        out_shape=[jax.ShapeDtypeStruct((n, HEAD_WIDTH), BF16),
                   jax.ShapeDtypeStruct((n, HEAD_WIDTH), BF16),
                   jax.ShapeDtypeStruct((HEAD_WIDTH, n), BF16),
                   jax.ShapeDtypeStruct((HEAD_WIDTH, n), BF16),
                   jax.ShapeDtypeStruct((n, HEAD_WIDTH), BF16),
                   jax.ShapeDtypeStruct((DA_HEADS * V_ROWS, n), BF16),
                   jax.ShapeDtypeStruct((n_prompt, HEAD_WIDTH), F32),
                   jax.ShapeDtypeStruct((n - n_prompt, HEAD_WIDTH), F32),
                   jax.ShapeDtypeStruct((n_prompt, HEAD_WIDTH), F32),
                   jax.ShapeDtypeStruct((n - n_prompt, HEAD_WIDTH), F32)],
        compiler_params=_cparams(("arbitrary",)),
        name="da_prep",
    )(proj, proj, proj, gq, gk, gmat)


def _lambda_of(lp, lam_init):
    a = jnp.sum(lp[0:1, :] * lp[1:2, :], axis=-1, keepdims=True)
    b = jnp.sum(lp[2:3, :] * lp[3:4, :], axis=-1, keepdims=True)
    return jnp.exp(a) - jnp.exp(b) + lam_init


def _subln(o, g, lam_init):
    ms = jnp.mean(o * o, axis=-1, keepdims=True)
    return (o * lax.rsqrt(ms + RMS_EPS)) * g * (1.0 - lam_init)


def _attn_p_kernel(lam_init, tile, qi_ref, kj_ref, bounded_ref, qlot_ref, qhit_ref, k_ref, vt_ref,
                   bound_ref, lp_ref, sg_ref, o_ref, m_scr, acc_scr):
    s = pl.program_id(0)
    qi = qi_ref[s]
    kj = kj_ref[s]
    bounded = bounded_ref[0] != 0
    n_maps = 2 * DA_HEADS

    @pl.when(kj == 0)
    def _():
        m_scr[...] = jnp.full(m_scr.shape, -jnp.inf, F32)
        acc_scr[...] = jnp.zeros(acc_scr.shape, F32)

    def scores(r, visible):
        h, c = divmod(r, 2)
        sl = slice(h * DA_V_DIM, (h + 1) * DA_V_DIM)
        sc = _dot(k_ref[:, sl], (qlot_ref, qhit_ref)[c][sl, :])
        return sc if visible is None else jnp.where(visible, sc, -jnp.inf)

    def sweep(visible):
        sc = scores(0, visible)
        for r in range(n_maps):
            nxt = scores(r + 1, visible) if r + 1 < n_maps else None
            h = r // 2
            m_prev = m_scr[r:r + 1, :]
            m_new = jnp.maximum(m_prev, jnp.max(sc, axis=0, keepdims=True))
            alpha = jnp.exp2(m_prev - m_new)
            e = jnp.exp2(sc - m_new).astype(BF16)
            acc_scr[r] = alpha * acc_scr[r] + _dot(vt_ref[h * V_ROWS:(h + 1) * V_ROWS, :], e)
            m_scr[r:r + 1, :] = m_new
            sc = nxt

    def sweep_bounded(visible):
        bound = bound_ref[0:1, 0:1]
        for r in range(n_maps):
            h = r // 2
            e = jnp.exp2(scores(r, None) - bound)
            if visible is not None:
                e = jnp.where(visible, e, 0.0)
            acc_scr[r] += _dot(vt_ref[h * V_ROWS:(h + 1) * V_ROWS, :], e.astype(BF16))

    def chunk_mask():
        shift = CHUNK.bit_length() - 1
        key_chunk = lax.broadcasted_iota(I32, (tile, tile), 0) >> shift
        qry_chunk = lax.broadcasted_iota(I32, (tile, tile), 1) >> shift
        return key_chunk <= qry_chunk

    off_diag = kj < qi
    on_diag = kj == qi
    unbounded = jnp.logical_not(bounded)

    @pl.when(jnp.logical_and(off_diag, bounded))
    def _():
        sweep_bounded(None)

    @pl.when(jnp.logical_and(on_diag, bounded))
    def _():
        sweep_bounded(chunk_mask())

    @pl.when(jnp.logical_and(off_diag, unbounded))
    def _():
        sweep(None)

    @pl.when(jnp.logical_and(on_diag, unbounded))
    def _():
        sweep(chunk_mask())

    @pl.when(on_diag)
    def _():
        lam = _lambda_of(lp_ref[...], lam_init)
        sg = sg_ref[...]
        outs = []
        for h in range(DA_HEADS):
            acc0 = acc_scr[2 * h]
            acc1 = acc_scr[2 * h + 1]
            a0 = acc0[0:DA_V_DIM, :] / acc0[DA_V_DIM:DA_V_DIM + 1, :]
            a1 = acc1[0:DA_V_DIM, :] / acc1[DA_V_DIM:DA_V_DIM + 1, :]
            outs.append(_subln((a0 - lam * a1).T, sg, lam_init))
        o_ref[...] = jnp.concatenate(outs, axis=1).astype(o_ref.dtype)


def _attn_prompt(qlot, qhit, kb, vt, bound, lp, sg, n_prompt, lam_init, tile):
    nq = n_prompt // tile
    qi_l, kj_l = [], []
    for i in range(nq):
        for j in range(i + 1):
            qi_l.append(i)
            kj_l.append(j)
    qi = jnp.asarray(qi_l, I32)
    kj = jnp.asarray(kj_l, I32)
    bounded = (bound <= ATTN_SAFE_BOUND).astype(I32).reshape(1)
    bound_row = jnp.broadcast_to(bound.astype(F32), (1, LANES))
    const = lambda shp: pl.BlockSpec(shp, lambda s, qi, kj, bd: (0, 0))
    grid_spec = pltpu.PrefetchScalarGridSpec(
        num_scalar_prefetch=3,
        grid=(len(qi_l),),
        in_specs=[pl.BlockSpec((HEAD_WIDTH, tile), lambda s, qi, kj, bd: (0, qi[s])),
                  pl.BlockSpec((HEAD_WIDTH, tile), lambda s, qi, kj, bd: (0, qi[s])),
                  pl.BlockSpec((tile, HEAD_WIDTH), lambda s, qi, kj, bd: (kj[s], 0)),
                  pl.BlockSpec((DA_HEADS * V_ROWS, tile), lambda s, qi, kj, bd: (0, kj[s])),
                  const((1, LANES)), const((4, DA_QK_DIM)), const((1, DA_V_DIM))],
        out_specs=pl.BlockSpec((tile, HEAD_WIDTH), lambda s, qi, kj, bd: (qi[s], 0)),
        scratch_shapes=[pltpu.VMEM((2 * DA_HEADS, tile), F32),
                        pltpu.VMEM((2 * DA_HEADS, V_ROWS, tile), F32)])
    return pl.pallas_call(
        functools.partial(_attn_p_kernel, lam_init, tile),
        grid_spec=grid_spec,
        out_shape=jax.ShapeDtypeStruct((n_prompt, HEAD_WIDTH), BF16),
        compiler_params=_cparams(("arbitrary",)),
        name="attn_p",
    )(qi, kj, bounded, qlot, qhit, kb, vt, bound_row, lp, sg)


def _attn_s_kernel(lam_init, qlo_ref, qhi_ref, kn_ref, vn_ref, ckt_ref, cv_ref, lp_ref, sg_ref,
                   o_ref):
    past = ckt_ref.shape[-1]
    lam = _lambda_of(lp_ref[...], lam_init)
    sg = sg_ref[...]
    outs = []
    for h in range(DA_HEADS):
        sl = slice(h * DA_V_DIM, (h + 1) * DA_V_DIM)
        kn = kn_ref[:, sl].astype(BF16)
        vc = cv_ref[0, pl.ds(h, past, stride=DA_HEADS), :].astype(BF16)
        vn = vn_ref[:, sl].astype(BF16)
        acc = None
        for c, q_ref in enumerate((qlo_ref, qhi_ref)):
            qh = q_ref[:, sl]
            kct = ckt_ref[0, h, c].astype(BF16)
            s_c = _dot(qh[:, c * DA_QK_DIM:(c + 1) * DA_QK_DIM], kct)
            s_n = _dot_nt(qh, kn)
            m = jnp.maximum(jnp.max(s_c, axis=-1, keepdims=True),
                            jnp.max(s_n, axis=-1, keepdims=True))
            e_c = jnp.exp2(s_c - m)
            e_n = jnp.exp2(s_n - m)
            l = jnp.sum(e_c, axis=-1, keepdims=True) + jnp.sum(e_n, axis=-1, keepdims=True)
            part = (_dot(e_c.astype(BF16), vc) + _dot(e_n.astype(BF16), vn)) / l
            acc = part if c == 0 else acc - lam * part
        outs.append(_subln(acc, sg, lam_init))
    o_ref[...] = jnp.concatenate(outs, axis=1).astype(o_ref.dtype)


def _attn_sample(qlo, qhi, ks, vs, cache_k, cache_v, lp, sg, n_prompt, lam_init):
    bs, past = cache_k.shape[0], cache_k.shape[1]
    cache_kt = jnp.transpose(cache_k, (0, 2, 3, 4, 1))
    cache_v = cache_v.reshape(bs, past * DA_HEADS, DA_V_DIM)
    t = ks.shape[0] // bs
    off = n_prompt // t
    return pl.pallas_call(
        functools.partial(_attn_s_kernel, lam_init),
        grid=(bs,),
        in_specs=[pl.BlockSpec((t, HEAD_WIDTH), lambda b: (off + b, 0)),
                  pl.BlockSpec((t, HEAD_WIDTH), lambda b: (off + b, 0)),
                  pl.BlockSpec((t, HEAD_WIDTH), lambda b: (b, 0)),
                  pl.BlockSpec((t, HEAD_WIDTH), lambda b: (b, 0)),
                  pl.BlockSpec((1, DA_HEADS, 2, DA_QK_DIM, past), lambda b: (b, 0, 0, 0, 0)),
                  pl.BlockSpec((1, past * DA_HEADS, DA_V_DIM), lambda b: (b, 0, 0)),
                  pl.BlockSpec((4, DA_QK_DIM), lambda b: (0, 0)),
                  pl.BlockSpec((1, DA_V_DIM), lambda b: (0, 0))],
        out_specs=pl.BlockSpec((t, HEAD_WIDTH), lambda b: (b, 0)),
        out_shape=jax.ShapeDtypeStruct((bs * t, HEAD_WIDTH), BF16),
        compiler_params=_cparams(("arbitrary",)),
        name="attn_s",
    )(qlo, qhi, ks, vs, cache_kt, cache_v, lp, sg)


def _pack_bf16_pairs(x):
    half = x.shape[1] // 2
    lo = lax.bitcast_convert_type(x[:, :half].astype(BF16).astype(F32), U32)
    hi = lax.bitcast_convert_type(x[:, half:].astype(BF16).astype(F32), U32)
    return (lo >> 16) | (hi & jnp.uint32(0xFFFF0000))


def _unpack_bf16_pairs(w):
    lo = lax.bitcast_convert_type(w << 16, F32).astype(BF16)
    hi = lax.bitcast_convert_type(w & jnp.uint32(0xFFFF0000), F32).astype(BF16)
    return lo, hi


def _mix_kernel(n_prompt_tiles, hg_ref, dap_ref, das_ref, xp_ref, xs_ref, wo_ref, g2_ref, wr_ref,
                br_ref, x1_ref, h2p_ref, route_ref, gates_ref, cnt_ref, base_scr):
    i = pl.program_id(0)

    @pl.when(i == 0)
    def _():
        base_scr[...] = jnp.zeros(base_scr.shape, F32)

    tm = xp_ref.shape[0]
    is_prompt = i < n_prompt_tiles
    da = jnp.where(is_prompt, dap_ref[...], das_ref[...])
    x = jnp.where(is_prompt, xp_ref[...], xs_ref[...])
    x1 = x + _dot(hg_ref[...], wo_ref[0:HEAD_WIDTH, :]) \
        + _dot(da, wo_ref[HEAD_WIDTH:2 * HEAD_WIDTH, :])
    x1_ref[...] = x1
    ms = jnp.mean(x1 * x1, axis=-1, keepdims=True)
    h2 = (x1 * lax.rsqrt(ms + RMS_EPS)) * g2_ref[...]
    h2p_ref[...] = _pack_bf16_pairs(h2)

    h_hi, h_lo = _split_bf16(h2)
    w_hi, w_lo = _split_bf16(wr_ref[...])
    logits = _dot(h_hi, w_hi) + _dot(h_lo, w_hi) + _dot(h_hi, w_lo) + br_ref[...]
    lane = lax.broadcasted_iota(I32, logits.shape, 1)
    lanef = lane.astype(F32)
    lg = jnp.where(lane < N_EXPERTS, logits, -jnp.inf)

    sels, vals, idxs = [], [], []
    for _ in range(TOP_K):
        mx = jnp.max(lg, axis=-1, keepdims=True)
        idx = jnp.min(jnp.where(lg == mx, lanef, float(LANES)), axis=-1, keepdims=True)
        sel = lanef == idx
        lg = jnp.where(sel, -jnp.inf, lg)
        sels.append(sel)
        vals.append(mx)
        idxs.append(idx)
    es = [jnp.exp(v - vals[0]) for v in vals]
    den = es[0] + es[1] + es[2] + es[3]

    onehot = jnp.zeros(logits.shape, F32)
    for sel in sels:
        onehot = onehot + sel.astype(F32)
    row = lax.broadcasted_iota(I32, (tm, tm), 0)
    col = lax.broadcasted_iota(I32, (tm, tm), 1)
    strict = (col < row).astype(BF16)
    before = _dot(strict, onehot.astype(BF16)) + base_scr[0:1, :]
    base_scr[0:1, :] = base_scr[0:1, :] + jnp.sum(onehot, axis=0, keepdims=True)

    route = jnp.zeros(logits.shape, F32)
    gates = jnp.zeros(logits.shape, F32)
    for r in range(TOP_K):
        rank = jnp.sum(jnp.where(sels[r], before, 0.0), axis=-1, keepdims=True)
        route = jnp.where(lane == r, idxs[r], route)
        route = jnp.where(lane == TOP_K + r, rank, route)
        gates = jnp.where(lane == r, es[r] / den, gates)
    route_ref[...] = route.astype(I32)
    gates_ref[...] = gates
    cnt_ref[...] = jnp.broadcast_to(base_scr[0:1, :], cnt_ref.shape).astype(I32)


def _mix(hg, da_p, da_s, x_p, x_s, wo_bf16, g2, wr_pad, br_pad, tm):
    d = x_p.shape[1]
    n = x_p.shape[0] + x_s.shape[0]
    npt = x_p.shape[0] // tm
    row = lambda w: pl.BlockSpec((tm, w), lambda i: (i, 0))
    p_row = lambda w: pl.BlockSpec((tm, w), lambda i: (jnp.minimum(i, npt - 1), 0))
    s_row = lambda w: pl.BlockSpec((tm, w), lambda i: (jnp.maximum(i - npt, 0), 0))
    const = lambda shp: pl.BlockSpec(shp, lambda i: (0, 0))
    return pl.pallas_call(
        functools.partial(_mix_kernel, npt),
        grid=(n // tm,),
        in_specs=[row(HEAD_WIDTH), p_row(HEAD_WIDTH), s_row(HEAD_WIDTH), p_row(d), s_row(d),
                  const((2 * HEAD_WIDTH, d)), const((1, d)), const((d, LANES)), const((1, LANES))],
        out_specs=[row(d), row(d // 2), row(LANES), row(LANES), const((8, LANES))],
        out_shape=[jax.ShapeDtypeStruct((n, d), F32),
                   jax.ShapeDtypeStruct((n, d // 2), U32),
                   jax.ShapeDtypeStruct((n, LANES), I32),
                   jax.ShapeDtypeStruct((n, LANES), F32),
                   jax.ShapeDtypeStruct((8, LANES), I32)],
        scratch_shapes=[pltpu.VMEM((8, LANES), F32)],
        compiler_params=_cparams(("arbitrary",)),
        name="mix",
    )(hg, da_p, da_s, x_p, x_s, wo_bf16, g2, wr_pad, br_pad)


def _dispatch_copy(h_ref, xs_ref, sem, dest_ref, r, kk):
    return pltpu.make_async_copy(h_ref.at[pl.ds(r, 1)],
                                 xs_ref.at[pl.ds(dest_ref[r * TOP_K + kk], 1)], sem)


def _dispatch_kernel(zinfo_ref, dest_ref, h_ref, xs_ref, zbuf, sem, zsem):
    tm = h_ref.shape[0]
    bm = zbuf.shape[0]
    nb = xs_ref.shape[0] // bm

    def zero_copy(start):
        return pltpu.make_async_copy(zbuf, xs_ref.at[pl.ds(pl.multiple_of(start, bm), bm)], zsem)

    def for_each_zero_block(fn):
        for e in range(N_EXPERTS):
            start = zinfo_ref[e]

            @pl.when(start >= 0)
            def _():
                fn(zero_copy(start))

        def tail(b, carry):
            fn(zero_copy(b * bm))
            return carry

        lax.fori_loop(zinfo_ref[N_EXPERTS], nb, tail, 0)

    @pl.when(pl.program_id(0) == 0)
    def _():
        zbuf[...] = jnp.zeros(zbuf.shape, zbuf.dtype)
        for_each_zero_block(lambda cp: cp.start())
        for_each_zero_block(lambda cp: cp.wait())

    def issue(r, carry):
        for kk in range(TOP_K):
            _dispatch_copy(h_ref, xs_ref, sem, dest_ref, r, kk).start()
        return carry

    def drain(r, carry):
        for kk in range(TOP_K):
            _dispatch_copy(h_ref, xs_ref, sem, dest_ref, r, kk).wait()
        return carry

    lax.fori_loop(0, tm, issue, 0)
    lax.fori_loop(0, tm, drain, 0)


def _dispatch(h2p, dest_flat, zinfo, n_rows, bm, tm):
    n, w = h2p.shape
    grid_spec = pltpu.PrefetchScalarGridSpec(
        num_scalar_prefetch=1,
        grid=(n // tm,),
        in_specs=[pl.BlockSpec((tm * TOP_K,), lambda i, z: (i,), memory_space=pltpu.SMEM),
                  pl.BlockSpec((tm, w), lambda i, z: (i, 0))],
        out_specs=pl.BlockSpec(memory_space=pl.ANY),
        scratch_shapes=[pltpu.VMEM((bm, w), h2p.dtype),
                        pltpu.SemaphoreType.DMA, pltpu.SemaphoreType.DMA])
    return pl.pallas_call(
        _dispatch_kernel,
        grid_spec=grid_spec,
        out_shape=jax.ShapeDtypeStruct((n_rows, w), h2p.dtype),
        compiler_params=_cparams(("arbitrary",)),
        name="dispatch",
    )(zinfo, dest_flat, h2p)


def _moe_kernel(n_parts, be_ref, nu_ref, parts_ref, xs_ref, wg_ref, wu_ref, bg_ref, bu_ref, wd_ref,
                bd_ref, o_ref, xb_scr):
    i = pl.program_id(0)
    j = pl.program_id(1)
    parts = parts_ref[i]
    used = parts > 0
    part_rows = o_ref.shape[0] // n_parts

    @pl.when(jnp.logical_and(j == 0, used))
    def _():
        half = xs_ref.shape[1]
        lo, hi = _unpack_bf16_pairs(xs_ref[...])
        xb_scr[:, 0:half] = lo
        xb_scr[:, half:2 * half] = hi
        o_ref[...] = jnp.broadcast_to(bd_ref[0], o_ref.shape)

    @pl.when(jnp.logical_and(j == 0, jnp.logical_not(used)))
    def _():
        o_ref[...] = jnp.zeros(o_ref.shape, F32)

    def experts_on(rows):
        xb = xb_scr[0:rows, :]
        gate = _dot(xb, wg_ref[0].astype(BF16)) + bg_ref[0]
        up = _dot(xb, wu_ref[0].astype(BF16)) + bu_ref[0]
        gate = jnp.minimum(gate, SWIGLU_LIMIT)
        up = jnp.clip(up, -SWIGLU_LIMIT, SWIGLU_LIMIT)
        glu = gate * _sigmoid(SWIGLU_ALPHA * gate)
        mid = ((up + 1.0) * glu).astype(BF16)
        o_ref[0:rows, :] += _dot(mid, wd_ref[0].astype(BF16))

    for p in range(1, n_parts + 1):
        @pl.when(parts == p)
        def _(p=p):
            experts_on(p * part_rows)


def _moe(xs, block_expert, n_used, block_parts, n_parts, w_gu, b_gu, w_down, b_down, bm, th):
    rows, half = xs.shape
    d = 2 * half
    n_e, _, two_h = w_gu.shape
    hid = two_h // 2
    nj = hid // th
    nb = rows // bm
    b_gu3 = b_gu.reshape(n_e, 1, two_h)
    b_dn3 = b_down.reshape(n_e, 1, d)

    def xrow(i, j, be, nu, bp):
        return (jnp.minimum(i, jnp.maximum(nu[0] - 1, 0)), 0)

    def hcol(i, j, nu):
        return jnp.where(i < nu[0], j, nj - 1)

    grid_spec = pltpu.PrefetchScalarGridSpec(
        num_scalar_prefetch=3,
        grid=(nb, nj),
        in_specs=[pl.BlockSpec((bm, half), xrow),
                  pl.BlockSpec((1, d, th), lambda i, j, be, nu, bp: (be[i], 0, hcol(i, j, nu))),
                  pl.BlockSpec((1, d, th),
                               lambda i, j, be, nu, bp: (be[i], 0, nj + hcol(i, j, nu))),
                  pl.BlockSpec((1, 1, th), lambda i, j, be, nu, bp: (be[i], 0, hcol(i, j, nu))),
                  pl.BlockSpec((1, 1, th),
                               lambda i, j, be, nu, bp: (be[i], 0, nj + hcol(i, j, nu))),
                  pl.BlockSpec((1, th, d), lambda i, j, be, nu, bp: (be[i], hcol(i, j, nu), 0)),
                  pl.BlockSpec((1, 1, d), lambda i, j, be, nu, bp: (be[i], 0, 0))],
        out_specs=pl.BlockSpec((bm, d), lambda i, j, be, nu, bp: (i, 0)),
        scratch_shapes=[pltpu.VMEM((bm, d), BF16)])
    return pl.pallas_call(
        functools.partial(_moe_kernel, n_parts),
        grid_spec=grid_spec,
        out_shape=jax.ShapeDtypeStruct((rows, d), F32),
        compiler_params=_cparams(("arbitrary", "arbitrary")),
        name="moe",
    )(block_expert, n_used, block_parts, xs, w_gu, w_gu, b_gu3, b_gu3, w_down, b_dn3)


def _combine_kernel(n_prompt_tiles, n_tiles, dest_ref, dest_next_ref, x1_ref, gates_ref, yb_ref,
                    yp_ref, ys_ref, buf, sem):
    i = pl.program_id(0)
    tm = x1_ref.shape[0]
    slot = i % 2

    def for_each_row_copy(d_ref, s, fn):
        def body(r, carry):
            for kk in range(TOP_K):
                fn(pltpu.make_async_copy(yb_ref.at[pl.ds(d_ref[r * TOP_K + kk], 1)],
                                         buf.at[s, kk, pl.ds(r, 1)], sem.at[s]))
            return carry

        lax.fori_loop(0, tm, body, 0)

    @pl.when(i == 0)
    def _():
        for_each_row_copy(dest_ref, 0, lambda cp: cp.start())

    @pl.when(i + 1 < n_tiles)
    def _():
        for_each_row_copy(dest_next_ref, 1 - slot, lambda cp: cp.start())

    for_each_row_copy(dest_ref, slot, lambda cp: cp.wait())

    g = gates_ref[...]
    y = x1_ref[...]
    for kk in range(TOP_K):
        y = y + g[:, kk:kk + 1] * buf[slot, kk]

    @pl.when(i < n_prompt_tiles)
    def _():
        yp_ref[...] = y

    @pl.when(i >= n_prompt_tiles)
    def _():
        ys_ref[...] = y


def _combine(x1, gates, dest_flat, yb, n_prompt, tm):
    n, d = x1.shape
    npt = n_prompt // tm
    n_tiles = n // tm
    return pl.pallas_call(
        functools.partial(_combine_kernel, npt, n_tiles),
        grid=(n_tiles,),
        in_specs=[pl.BlockSpec((tm * TOP_K,), lambda i: (i,), memory_space=pltpu.SMEM),
                  pl.BlockSpec((tm * TOP_K,), lambda i: (jnp.minimum(i + 1, n_tiles - 1),),
                               memory_space=pltpu.SMEM),
                  pl.BlockSpec((tm, d), lambda i: (i, 0)),
                  pl.BlockSpec((tm, LANES), lambda i: (i, 0)),
                  pl.BlockSpec(memory_space=pl.ANY)],
        out_specs=[pl.BlockSpec((tm, d), lambda i: (jnp.minimum(i, npt - 1), 0)),
                   pl.BlockSpec((tm, d), lambda i: (jnp.maximum(i - npt, 0), 0))],
        out_shape=[jax.ShapeDtypeStruct((n_prompt, d), F32),
                   jax.ShapeDtypeStruct((n - n_prompt, d), F32)],
        scratch_shapes=[pltpu.VMEM((2, TOP_K, tm, d), F32), pltpu.SemaphoreType.DMA((2,))],
        compiler_params=_cparams(("arbitrary",)),
        name="combine",
    )(dest_flat, dest_flat, x1, gates, yb)


def _group_matrix(group):
    r = jnp.arange(256) // group
    return (r[:, None] == r[None, :]).astype(BF16)


def _tile(n, want):
    t = min(want, n)
    while n % t:
        t //= 2
    return t


def _layer(x_prompt, x_sample, cache_k, cache_v, state_hgrn, attn_norm_g, w_in, lb, hg_norm_g,
           da_q_norm_g, da_k_norm_g, da_lambda, da_subln_g, w_out, ffn_norm_g, w_router,
           b_router, w_gu, b_gu, w_down, b_down, lam_init):
    bp, s, d = x_prompt.shape
    bs, t, _ = x_sample.shape
    assert bp == 1 and t == HG_TILE and s % HG_TILE == 0
    n_p = bp * s
    n = n_p + bs * t
    x_p = x_prompt.reshape(n_p, d)
    x_s = x_sample.reshape(bs * t, d)
    row_gcd = math.gcd(n_p, bs * t)

    proj = _proj(x_p, x_s, attn_norm_g.reshape(1, d), w_in.astype(BF16), _tile(row_gcd, PROJ_TM),
                 PROJ_TN)

    hg_group = _tile(math.gcd(n_p // HG_TILE, bs), HG_GROUP)
    hg, s_all = _hgrn(proj, lb.reshape(1, HEAD_WIDTH), hg_norm_g.reshape(1, HG_DIM), state_hgrn,
                      n_p, hg_group)

    tm_da = _tile(row_gcd, DA_TM)
    gq = jnp.tile(da_q_norm_g, 2 * DA_HEADS).reshape(1, HEAD_WIDTH)
    gk = jnp.tile(da_k_norm_g, 2 * DA_HEADS).reshape(1, HEAD_WIDTH)
    qlo, qhi, qlot, qhit, kb, vt, k_p, k_s, v_p, v_s = _da_prep(proj, gq, gk,
                                                                _group_matrix(DA_QK_DIM),
                                                    n_p, tm_da)
    sg = da_subln_g.reshape(1, DA_V_DIM)
    score_bound = (DA_QK_DIM ** 0.5 * LOG2_E) * jnp.max(jnp.abs(da_q_norm_g)) \
        * jnp.max(jnp.abs(da_k_norm_g))
    da_p = _attn_prompt(qlot, qhit, kb, vt, score_bound, da_lambda, sg, n_p, lam_init,
                        _tile(n_p, ATTN_TILE))
    past = cache_k.shape[1]
    da_s = _attn_sample(qlo, qhi, k_s, v_s, cache_k, cache_v, da_lambda, sg, n_p, lam_init)

    wr_pad = jnp.pad(w_router, ((0, 0), (0, LANES - N_EXPERTS)))
    br_pad = jnp.pad(b_router, (0, LANES - N_EXPERTS)).reshape(1, LANES)
    x1, h2p, route, gates, cnt = _mix(hg, da_p, da_s, x_p, x_s, w_out.astype(BF16),
                                      ffn_norm_g.reshape(1, d), wr_pad, br_pad,
                                      _tile(row_gcd, MIX_TM))

    bm = MOE_BM
    counts = cnt[0, :N_EXPERTS]
    padded = (counts + bm - 1) // bm * bm
    pend = jnp.cumsum(padded)
    pstart = pend - padded
    dest = (pstart[route[:, 0:TOP_K]] + route[:, TOP_K:2 * TOP_K]).reshape(-1).astype(I32)
    nb = (n * TOP_K + N_EXPERTS * (bm - 1)) // bm
    n_used = (pend[-1] // bm).astype(I32)
    blk = jnp.minimum(jnp.arange(nb, dtype=I32), jnp.maximum(n_used - 1, 0))
    block_expert = jnp.minimum(jnp.sum(pend[None, :] <= (blk * bm)[:, None], axis=1),
                               N_EXPERTS - 1).astype(I32)
    zinfo = jnp.concatenate([jnp.where(padded > 0, pend - bm, -1), n_used[None]]).astype(I32)
    n_parts = MOE_PARTS if bm % (8 * MOE_PARTS) == 0 else 1
    blk_all = jnp.arange(nb, dtype=I32)
    valid_rows = jnp.clip(counts[block_expert] - (blk_all * bm - pstart[block_expert]), 0, bm)
    part_rows = bm // n_parts
    block_parts = jnp.where(blk_all < n_used, (valid_rows + part_rows - 1) // part_rows,
                            0).astype(I32)

    xs = _dispatch(h2p, dest, zinfo, nb * bm, bm, _tile(n, DISPATCH_TM))
    yb = _moe(xs, block_expert, n_used.reshape(1), block_parts, n_parts, w_gu, b_gu, w_down,
              b_down, bm, min(MOE_TH, w_down.shape[1]))
    y_p, y_s = _combine(x1, gates, dest, yb, n_p, _tile(row_gcd, COMBINE_TM))

    return (y_p.reshape(bp, s, d), y_s.reshape(bs, t, d),
            k_p.reshape(bp, s, DA_HEADS, 2, DA_QK_DIM), v_p.reshape(bp, s, DA_HEADS, DA_V_DIM),
            s_all[0:1],
            k_s.reshape(bs, t, DA_HEADS, 2, DA_QK_DIM), v_s.reshape(bs, t, DA_HEADS, DA_V_DIM),
            s_all[hg_group:])


def kernel(x_prompt, x_sample, cache_k, cache_v, state_hgrn, attn_norm_g, w_in, hg_lb_logits,
           hg_norm_g, da_q_norm_g, da_k_norm_g, da_lambda, da_subln_g, w_out, ffn_norm_g,
           w_router, b_router, w_gu, b_gu, w_down, b_down):
    depth = w_in.shape[0]
    assert depth == 1, "single-layer step"
    lb_all = jnp.cumsum(jax.nn.softmax(hg_lb_logits.astype(F32), axis=0), axis=0)
    lam_init = 0.8 - 0.6 * math.exp(-0.3 * 0)
    outs = _layer(x_prompt, x_sample, cache_k[0], cache_v[0], state_hgrn[0], attn_norm_g[0],
                  w_in[0], lb_all[0], hg_norm_g[0], da_q_norm_g[0], da_k_norm_g[0], da_lambda[0],
                  da_subln_g[0], w_out[0], ffn_norm_g[0], w_router[0], b_router[0], w_gu[0],
                  b_gu[0], w_down[0], b_down[0], lam_init)
    y_p, y_s, k_p, v_p, s_p, k_s, v_s, s_s = outs
    return (y_p, y_s, k_p[None], v_p[None], s_p[None], k_s[None], v_s[None], s_s[None])
```

```python
import functools
import math

import jax
import jax.numpy as jnp
from jax import lax
from jax.experimental import pallas as pl
from jax.experimental.pallas import tpu as pltpu

F32 = jnp.float32
BF16 = jnp.bfloat16
I32 = jnp.int32
U32 = jnp.uint32

RMS_EPS = 1e-6
CHUNK = 64
HG_HEADS = 8
HG_DIM = 128
DA_HEADS = 8
DA_QK_DIM = 64
DA_V_DIM = 128
HEAD_WIDTH = 1024
N_EXPERTS = 32
TOP_K = 4
SWIGLU_LIMIT = 7.0
SWIGLU_ALPHA = 1.702
LANES = 128
HG_TILE = 64
HG_SUB = 16
HG_GROUP = 4
EXP_CLAMP = 80.0
LOG2_E = 1.4426950408889634
V_COLS = 2 * DA_V_DIM
ATTN_SAFE_BOUND = 50.0
VMEM_LIMIT = 56 * 1024 * 1024
PROJ_TM, PROJ_TN = 256, 3584
DA_TM = 256
ATTN_TILE = 512
MIX_TM = 256
DISPATCH_TM = 256
MOE_BM, MOE_TH = 1024, 256
MOE_PARTS = 4
COMBINE_TM = 128


def _cparams(sem, vmem=VMEM_LIMIT):
    return pltpu.CompilerParams(dimension_semantics=sem, vmem_limit_bytes=vmem)


def _dot(a, b):
    return jnp.dot(a, b, preferred_element_type=F32)


def _dot_nt(a, b):
    return lax.dot_general(a, b, (((1,), (1,)), ((), ())), preferred_element_type=F32)


def _split_bf16(x):
    hi = x.astype(BF16)
    lo = (x - hi.astype(F32)).astype(BF16)
    return hi, lo


def _sigmoid(x):
    return 1.0 / (1.0 + jnp.exp(-x))


def _group_sum(sq, gmat):
    outs = []
    for c in range(sq.shape[1] // 256):
        hi, lo = _split_bf16(sq[:, c * 256:(c + 1) * 256])
        outs.append(_dot(hi, gmat) + _dot(lo, gmat))
    return jnp.concatenate(outs, axis=1)


def _proj_kernel(n_prompt_tiles, xp_ref, xs_ref, g_ref, w_ref, o_ref):
    is_prompt = pl.program_id(1) < n_prompt_tiles
    x = jnp.where(is_prompt, xp_ref[...], xs_ref[...])
    ms = jnp.mean(x * x, axis=-1, keepdims=True)
    h = ((x * lax.rsqrt(ms + RMS_EPS)) * g_ref[...]).astype(BF16)
    o_ref[...] = _dot(h, w_ref[...])


def _proj(x_p, x_s, g, w_bf16, tm, tn):
    d = x_p.shape[1]
    n = x_p.shape[0] + x_s.shape[0]
    npt = x_p.shape[0] // tm
    cols = w_bf16.shape[1]
    return pl.pallas_call(
        functools.partial(_proj_kernel, npt),
        grid=(cols // tn, n // tm),
        in_specs=[pl.BlockSpec((tm, d), lambda j, i: (jnp.minimum(i, npt - 1), 0)),
                  pl.BlockSpec((tm, d), lambda j, i: (jnp.maximum(i - npt, 0), 0)),
                  pl.BlockSpec((1, d), lambda j, i: (0, 0)),
                  pl.BlockSpec((d, tn), lambda j, i: (0, j))],
        out_specs=pl.BlockSpec((tm, tn), lambda j, i: (i, j)),
        out_shape=jax.ShapeDtypeStruct((n, cols), F32),
        compiler_params=_cparams(("arbitrary", "arbitrary")),
        name="proj",
    )(x_p, x_s, g, w_bf16)


def _hgrn_chunk(q_raw, f_raw, v, g_raw, lb, ng, states):
    q = q_raw * _sigmoid(q_raw)
    f = lb + (1.0 - lb) * _sigmoid(f_raw)
    logf = jnp.log(f)
    k = 1.0 - f
    gate = _sigmoid(g_raw)

    row = lax.broadcasted_iota(I32, (HG_TILE, HG_TILE), 0)
    col = lax.broadcasted_iota(I32, (HG_TILE, HG_TILE), 1)
    causal = col <= row
    tri = causal.astype(BF16)
    hi, lo = _split_bf16(logf)
    b = _dot(tri, hi) + _dot(tri, lo)
    b_last = b[HG_TILE - 1:HG_TILE, :]
    q_dec = q * jnp.exp(b)
    k_dec = k * jnp.exp(b_last - b)

    outs, new_states = [], []
    for h in range(HG_HEADS):
        sl = slice(h * HG_DIM, (h + 1) * HG_DIM)
        st = states[h]
        bh = b[:, sl]
        qh = q[:, sl]
        kh = k[:, sl]
        vh = v[:, sl].astype(BF16)
        o = _dot_nt(q_dec[:, sl].astype(BF16), st.astype(BF16))
        rows = []
        for i in range(HG_TILE // HG_SUB):
            r0 = i * HG_SUB
            base = bh[r0 - 1:r0, :] if i else jnp.zeros((1, HG_DIM), F32)
            qi = qh[r0:r0 + HG_SUB, :] * jnp.exp(bh[r0:r0 + HG_SUB, :] - base)
            ki = kh * jnp.exp(jnp.minimum(base - bh, EXP_CLAMP))
            rows.append(_dot_nt(qi.astype(BF16), ki.astype(BF16)))
        scores = jnp.where(causal, jnp.concatenate(rows, axis=0), 0.0)
        o = o + _dot(scores.astype(BF16), vh)
        new_states.append(st * jnp.exp(b_last[:, sl])
                          + _dot(v[:, sl].T.astype(BF16), k_dec[:, sl].astype(BF16)))
        ms = jnp.mean(o * o, axis=-1, keepdims=True)
        outs.append((o * lax.rsqrt(ms + RMS_EPS)) * ng)
    return jnp.concatenate(outs, axis=1) * gate, new_states


def _hgrn_kernel(n_prompt_steps, group, q_ref, f_ref, i_ref, g_ref, lb_ref, ng_ref, s0_ref,
                 o_ref, sout_ref, st_scr):
    t = pl.program_id(0)
    lb = lb_ref[...]
    ng = ng_ref[...]

    def chunk(g, states):
        rows = slice(g * HG_TILE, (g + 1) * HG_TILE)
        out, states = _hgrn_chunk(q_ref[rows, :], f_ref[rows, :], i_ref[rows, :], g_ref[rows, :],
                                  lb, ng, states)
        o_ref[rows, :] = out.astype(o_ref.dtype)
        return states

    @pl.when(t == 0)
    def _():
        st_scr[...] = jnp.zeros(st_scr.shape, F32)

    @pl.when(t < n_prompt_steps)
    def _():
        states = [st_scr[h] for h in range(HG_HEADS)]
        for g in range(group):
            states = chunk(g, states)
        for h in range(HG_HEADS):
            st_scr[h] = states[h]

    @pl.when(t == n_prompt_steps - 1)
    def _():
        for h in range(HG_HEADS):
            sout_ref[0, h] = st_scr[h].T
        if group > 1:
            sout_ref[1:group] = jnp.zeros((group - 1,) + tuple(sout_ref.shape[1:]), F32)

    @pl.when(t >= n_prompt_steps)
    def _():
        for g in range(group):
            states = chunk(g, [s0_ref[g, h].T for h in range(HG_HEADS)])
            for h in range(HG_HEADS):
                sout_ref[g, h] = states[h].T


def _hgrn(proj, lb, norm_g, state_s, n_prompt, group):
    n = proj.shape[0]
    rows = group * HG_TILE
    n_prompt_steps = n_prompt // rows
    s0_all = jnp.concatenate([jnp.zeros((group,) + state_s.shape[1:], F32), state_s], axis=0)
    seq = lambda t: (jnp.maximum(t - (n_prompt_steps - 1), 0), 0, 0, 0)
    col_spec = lambda c: pl.BlockSpec((rows, HEAD_WIDTH), lambda t, c=c: (t, c))
    return pl.pallas_call(
        functools.partial(_hgrn_kernel, n_prompt_steps, group),
        grid=(n // rows,),
        in_specs=[col_spec(0), col_spec(1), col_spec(2), col_spec(3),
                  pl.BlockSpec((1, HEAD_WIDTH), lambda t: (0, 0)),
                  pl.BlockSpec((1, HG_DIM), lambda t: (0, 0)),
                  pl.BlockSpec((group, HG_HEADS, HG_DIM, HG_DIM), seq)],
        out_specs=[pl.BlockSpec((rows, HEAD_WIDTH), lambda t: (t, 0)),
                   pl.BlockSpec((group, HG_HEADS, HG_DIM, HG_DIM), seq)],
        out_shape=[jax.ShapeDtypeStruct((n, HEAD_WIDTH), BF16),
                   jax.ShapeDtypeStruct(s0_all.shape, F32)],
        scratch_shapes=[pltpu.VMEM((HG_HEADS, HG_DIM, HG_DIM), F32)],
        compiler_params=_cparams(("arbitrary",)),
        name="hgrn",
    )(proj, proj, proj, proj, lb, norm_g, s0_all)


def _da_prep_kernel(n_prompt_tiles, dq_ref, dk_ref, dv_ref, gq_ref, gk_ref, gmat_ref,
                    qlo_ref, qhi_ref, kt_ref, vx_ref, kp_ref, ks_ref, vp_ref, vs_ref):
    i = pl.program_id(0)
    gmat = gmat_ref[...]
    inv = 1.0 / DA_QK_DIM

    dq = dq_ref[...]
    qn = dq * lax.rsqrt(_group_sum(dq * dq, gmat) * inv + RMS_EPS) * gq_ref[...]
    qn = qn * (DA_QK_DIM ** -0.5 * LOG2_E)
    lane = lax.broadcasted_iota(I32, qn.shape, 1)
    first = (lane & DA_QK_DIM) == 0
    qlo_ref[...] = jnp.where(first, qn, 0.0).astype(BF16)
    qhi_ref[...] = jnp.where(first, 0.0, qn).astype(BF16)

    dk = dk_ref[...]
    kn = dk * lax.rsqrt(_group_sum(dk * dk, gmat) * inv + RMS_EPS) * gk_ref[...]
    v = dv_ref[...]
    kt_ref[...] = kn.T.astype(BF16)
    ones_col = (lax.broadcasted_iota(I32, (v.shape[0], V_COLS - DA_V_DIM), 1) == 0).astype(F32)
    cols = []
    for h in range(DA_HEADS):
        cols += [v[:, h * DA_V_DIM:(h + 1) * DA_V_DIM], ones_col]
    vx_ref[...] = jnp.concatenate(cols, axis=1).astype(BF16)

    @pl.when(i < n_prompt_tiles)
    def _():
        kp_ref[...] = kn
        vp_ref[...] = v

    @pl.when(i >= n_prompt_tiles)
    def _():
        ks_ref[...] = kn
        vs_ref[...] = v


def _da_prep(proj, gq, gk, gmat, n_prompt, tm):
    n = proj.shape[0]
    npt = n_prompt // tm
    full = lambda: pl.BlockSpec((tm, HEAD_WIDTH), lambda i: (i, 0))
    p_spec = lambda: pl.BlockSpec((tm, HEAD_WIDTH), lambda i: (jnp.minimum(i, npt - 1), 0))
    s_spec = lambda: pl.BlockSpec((tm, HEAD_WIDTH), lambda i: (jnp.maximum(i - npt, 0), 0))
    vec = lambda: pl.BlockSpec((1, HEAD_WIDTH), lambda i: (0, 0))
    return pl.pallas_call(
        functools.partial(_da_prep_kernel, npt),
        grid=(n // tm,),
        in_specs=[pl.BlockSpec((tm, HEAD_WIDTH), lambda i: (i, 4)),
                  pl.BlockSpec((tm, HEAD_WIDTH), lambda i: (i, 5)),
                  pl.BlockSpec((tm, HEAD_WIDTH), lambda i: (i, 6)),
                  vec(), vec(),
                  pl.BlockSpec((256, 256), lambda i: (0, 0))],
        out_specs=[full(), full(),
                   pl.BlockSpec((HEAD_WIDTH, tm), lambda i: (0, i)),
                   pl.BlockSpec((tm, DA_HEADS * V_COLS), lambda i: (i, 0)),
                   p_spec(), s_spec(), p_spec(), s_spec()],
        out_shape=[jax.ShapeDtypeStruct((n, HEAD_WIDTH), BF16),
                   jax.ShapeDtypeStruct((n, HEAD_WIDTH), BF16),
                   jax.ShapeDtypeStruct((HEAD_WIDTH, n), BF16),
                   jax.ShapeDtypeStruct((n, DA_HEADS * V_COLS), BF16),
                   jax.ShapeDtypeStruct((n_prompt, HEAD_WIDTH), F32),
                   jax.ShapeDtypeStruct((n - n_prompt, HEAD_WIDTH), F32),
                   jax.ShapeDtypeStruct((n_prompt, HEAD_WIDTH), F32),
                   jax.ShapeDtypeStruct((n - n_prompt, HEAD_WIDTH), F32)],
        compiler_params=_cparams(("arbitrary",)),
        name="da_prep",
    )(proj, proj, proj, gq, gk, gmat)


def _lambda_of(lp, lam_init):
    a = jnp.sum(lp[0:1, :] * lp[1:2, :], axis=-1, keepdims=True)
    b = jnp.sum(lp[2:3, :] * lp[3:4, :], axis=-1, keepdims=True)
    return jnp.exp(a) - jnp.exp(b) + lam_init


def _subln(o, g, lam_init):
    ms = jnp.mean(o * o, axis=-1, keepdims=True)
    return (o * lax.rsqrt(ms + RMS_EPS)) * g * (1.0 - lam_init)


def _attn_q_kernel(lam_init, tile, qi_ref, kj_ref, bounded_ref, qlo_ref, qhi_ref, kt_ref, vx_ref,
                   bound_ref, lp_ref, sg_ref, o_ref, m_scr, acc_scr):
    s = pl.program_id(0)
    qi = qi_ref[s]
    kj = kj_ref[s]
    bounded = bounded_ref[0] != 0
    n_maps = 2 * DA_HEADS

    @pl.when(kj == 0)
    def _():
        m_scr[...] = jnp.full(m_scr.shape, -jnp.inf, F32)
        acc_scr[...] = jnp.zeros(acc_scr.shape, F32)

    def scores(r):
        h, c = divmod(r, 2)
        sl = slice(h * DA_V_DIM, (h + 1) * DA_V_DIM)
        return _dot((qlo_ref, qhi_ref)[c][:, sl], kt_ref[sl, :])

    def values(r):
        h = r // 2
        return vx_ref[:, h * V_COLS:(h + 1) * V_COLS]

    def sweep_bounded(visible):
        bound = bound_ref[0:1, 0:1]
        for r in range(n_maps):
            e = jnp.exp2(scores(r) - bound)
            if visible is not None:
                e = jnp.where(visible, e, 0.0)
            acc_scr[r] += _dot(e.astype(BF16), values(r))

    def sweep_online(visible):
        for r in range(n_maps):
            sc = scores(r)
            if visible is not None:
                sc = jnp.where(visible, sc, -jnp.inf)
            m_prev = m_scr[r]
            m_new = jnp.maximum(m_prev, jnp.max(sc, axis=-1, keepdims=True))
            alpha = jnp.exp2(m_prev - m_new)
            e = jnp.exp2(sc - m_new)
            acc_scr[r] = alpha * acc_scr[r] + _dot(e.astype(BF16), values(r))
            m_scr[r] = m_new

    def chunk_mask():
        shift = CHUNK.bit_length() - 1
        qry_chunk = lax.broadcasted_iota(I32, (tile, tile), 0) >> shift
        key_chunk = lax.broadcasted_iota(I32, (tile, tile), 1) >> shift
        return key_chunk <= qry_chunk

    off_diag = kj < qi
    on_diag = kj == qi
    unbounded = jnp.logical_not(bounded)

    @pl.when(jnp.logical_and(off_diag, bounded))
    def _():
        sweep_bounded(None)

    @pl.when(jnp.logical_and(on_diag, bounded))
    def _():
        sweep_bounded(chunk_mask())

    @pl.when(jnp.logical_and(off_diag, unbounded))
    def _():
        sweep_online(None)

    @pl.when(jnp.logical_and(on_diag, unbounded))
    def _():
        sweep_online(chunk_mask())

    @pl.when(on_diag)
    def _():
        lam = _lambda_of(lp_ref[...], lam_init)
        sg = sg_ref[...]
        outs = []
        for h in range(DA_HEADS):
            acc0 = acc_scr[2 * h]
            acc1 = acc_scr[2 * h + 1]
            a0 = acc0[:, 0:DA_V_DIM] / acc0[:, DA_V_DIM:DA_V_DIM + 1]
            a1 = acc1[:, 0:DA_V_DIM] / acc1[:, DA_V_DIM:DA_V_DIM + 1]
            outs.append(_subln(a0 - lam * a1, sg, lam_init))
        o_ref[...] = jnp.concatenate(outs, axis=1).astype(o_ref.dtype)


def _attn_prompt_q(qlo, qhi, kt, vx, bound, lp, sg, n_prompt, lam_init, tile):
    nq = n_prompt // tile
    qi_l, kj_l = [], []
    for i in range(nq):
        for j in range(i + 1):
            qi_l.append(i)
            kj_l.append(j)
    qi = jnp.asarray(qi_l, I32)
    kj = jnp.asarray(kj_l, I32)
    bounded = (bound <= ATTN_SAFE_BOUND).astype(I32).reshape(1)
    bound_row = jnp.broadcast_to(bound.astype(F32), (1, LANES))
    const = lambda shp: pl.BlockSpec(shp, lambda s, qi, kj, bd: (0, 0))
    grid_spec = pltpu.PrefetchScalarGridSpec(
        num_scalar_prefetch=3,
        grid=(len(qi_l),),
        in_specs=[pl.BlockSpec((tile, HEAD_WIDTH), lambda s, qi, kj, bd: (qi[s], 0)),
                  pl.BlockSpec((tile, HEAD_WIDTH), lambda s, qi, kj, bd: (qi[s], 0)),
                  pl.BlockSpec((HEAD_WIDTH, tile), lambda s, qi, kj, bd: (0, kj[s])),
                  pl.BlockSpec((tile, DA_HEADS * V_COLS), lambda s, qi, kj, bd: (kj[s], 0)),
                  const((1, LANES)), const((4, DA_QK_DIM)), const((1, DA_V_DIM))],
        out_specs=pl.BlockSpec((tile, HEAD_WIDTH), lambda s, qi, kj, bd: (qi[s], 0)),
        scratch_shapes=[pltpu.VMEM((2 * DA_HEADS, tile, 1), F32),
                        pltpu.VMEM((2 * DA_HEADS, tile, V_COLS), F32)])
    return pl.pallas_call(
        functools.partial(_attn_q_kernel, lam_init, tile),
        grid_spec=grid_spec,
        out_shape=jax.ShapeDtypeStruct((n_prompt, HEAD_WIDTH), BF16),
        compiler_params=_cparams(("arbitrary",)),
        name="attn_q",
    )(qi, kj, bounded, qlo, qhi, kt, vx, bound_row, lp, sg)


def _attn_s_kernel(lam_init, qlo_ref, qhi_ref, kn_ref, vn_ref, ckt_ref, cv_ref, lp_ref, sg_ref,
                   o_ref):
    past = ckt_ref.shape[-1]
    lam = _lambda_of(lp_ref[...], lam_init)
    sg = sg_ref[...]
    outs = []
    for h in range(DA_HEADS):
        sl = slice(h * DA_V_DIM, (h + 1) * DA_V_DIM)
        kn = kn_ref[:, sl].astype(BF16)
        vc = cv_ref[0, pl.ds(h, past, stride=DA_HEADS), :].astype(BF16)
        vn = vn_ref[:, sl].astype(BF16)
        acc = None
        for c, q_ref in enumerate((qlo_ref, qhi_ref)):
            qh = q_ref[:, sl]
            kct = ckt_ref[0, h, c].astype(BF16)
            s_c = _dot(qh[:, c * DA_QK_DIM:(c + 1) * DA_QK_DIM], kct)
            s_n = _dot_nt(qh, kn)
            m = jnp.maximum(jnp.max(s_c, axis=-1, keepdims=True),
                            jnp.max(s_n, axis=-1, keepdims=True))
            e_c = jnp.exp2(s_c - m)
            e_n = jnp.exp2(s_n - m)
            l = jnp.sum(e_c, axis=-1, keepdims=True) + jnp.sum(e_n, axis=-1, keepdims=True)
            part = (_dot(e_c.astype(BF16), vc) + _dot(e_n.astype(BF16), vn)) / l
            acc = part if c == 0 else acc - lam * part
        outs.append(_subln(acc, sg, lam_init))
    o_ref[...] = jnp.concatenate(outs, axis=1).astype(o_ref.dtype)


def _attn_sample(qlo, qhi, ks, vs, cache_k, cache_v, lp, sg, n_prompt, lam_init):
    bs, past = cache_k.shape[0], cache_k.shape[1]
    cache_kt = jnp.transpose(cache_k, (0, 2, 3, 4, 1))
    cache_v = cache_v.reshape(bs, past * DA_HEADS, DA_V_DIM)
    t = ks.shape[0] // bs
    off = n_prompt // t
    return pl.pallas_call(
        functools.partial(_attn_s_kernel, lam_init),
        grid=(bs,),
        in_specs=[pl.BlockSpec((t, HEAD_WIDTH), lambda b: (off + b, 0)),
                  pl.BlockSpec((t, HEAD_WIDTH), lambda b: (off + b, 0)),
                  pl.BlockSpec((t, HEAD_WIDTH), lambda b: (b, 0)),
                  pl.BlockSpec((t, HEAD_WIDTH), lambda b: (b, 0)),
                  pl.BlockSpec((1, DA_HEADS, 2, DA_QK_DIM, past), lambda b: (b, 0, 0, 0, 0)),
                  pl.BlockSpec((1, past * DA_HEADS, DA_V_DIM), lambda b: (b, 0, 0)),
                  pl.BlockSpec((4, DA_QK_DIM), lambda b: (0, 0)),
                  pl.BlockSpec((1, DA_V_DIM), lambda b: (0, 0))],
        out_specs=pl.BlockSpec((t, HEAD_WIDTH), lambda b: (b, 0)),
        out_shape=jax.ShapeDtypeStruct((bs * t, HEAD_WIDTH), BF16),
        compiler_params=_cparams(("arbitrary",)),
        name="attn_s",
    )(qlo, qhi, ks, vs, cache_kt, cache_v, lp, sg)


def _pack_bf16_pairs(x):
    half = x.shape[1] // 2
    lo = lax.bitcast_convert_type(x[:, :half].astype(BF16).astype(F32), U32)
    hi = lax.bitcast_convert_type(x[:, half:].astype(BF16).astype(F32), U32)
    return (lo >> 16) | (hi & jnp.uint32(0xFFFF0000))


def _unpack_bf16_pairs(w):
    lo = lax.bitcast_convert_type(w << 16, F32).astype(BF16)
    hi = lax.bitcast_convert_type(w & jnp.uint32(0xFFFF0000), F32).astype(BF16)
    return lo, hi


def _mix_kernel(n_prompt_tiles, hg_ref, dap_ref, das_ref, xp_ref, xs_ref, wo_ref, g2_ref, wr_ref,
                br_ref, x1_ref, h2p_ref, route_ref, gates_ref, cnt_ref, base_scr):
    i = pl.program_id(0)

    @pl.when(i == 0)
    def _():
        base_scr[...] = jnp.zeros(base_scr.shape, F32)

    tm = xp_ref.shape[0]
    is_prompt = i < n_prompt_tiles
    da = jnp.where(is_prompt, dap_ref[...], das_ref[...])
    x = jnp.where(is_prompt, xp_ref[...], xs_ref[...])
    x1 = x + _dot(hg_ref[...], wo_ref[0:HEAD_WIDTH, :]) \
        + _dot(da, wo_ref[HEAD_WIDTH:2 * HEAD_WIDTH, :])
    x1_ref[...] = x1
    ms = jnp.mean(x1 * x1, axis=-1, keepdims=True)
    h2 = (x1 * lax.rsqrt(ms + RMS_EPS)) * g2_ref[...]
    h2p_ref[...] = _pack_bf16_pairs(h2)

    h_hi, h_lo = _split_bf16(h2)
    w_hi, w_lo = _split_bf16(wr_ref[...])
    logits = _dot(h_hi, w_hi) + _dot(h_lo, w_hi) + _dot(h_hi, w_lo) + br_ref[...]
    lane = lax.broadcasted_iota(I32, logits.shape, 1)
    lanef = lane.astype(F32)
    lg = jnp.where(lane < N_EXPERTS, logits, -jnp.inf)

    sels, vals, idxs = [], [], []
    for _ in range(TOP_K):
        mx = jnp.max(lg, axis=-1, keepdims=True)
        idx = jnp.min(jnp.where(lg == mx, lanef, float(LANES)), axis=-1, keepdims=True)
        sel = lanef == idx
        lg = jnp.where(sel, -jnp.inf, lg)
        sels.append(sel)
        vals.append(mx)
        idxs.append(idx)
    es = [jnp.exp(v - vals[0]) for v in vals]
    den = es[0] + es[1] + es[2] + es[3]

    onehot = jnp.zeros(logits.shape, F32)
    for sel in sels:
        onehot = onehot + sel.astype(F32)
    row = lax.broadcasted_iota(I32, (tm, tm), 0)
    col = lax.broadcasted_iota(I32, (tm, tm), 1)
    strict = (col < row).astype(BF16)
    before = _dot(strict, onehot.astype(BF16)) + base_scr[0:1, :]
    base_scr[0:1, :] = base_scr[0:1, :] + jnp.sum(onehot, axis=0, keepdims=True)

    route = jnp.zeros(logits.shape, F32)
    gates = jnp.zeros(logits.shape, F32)
    for r in range(TOP_K):
        rank = jnp.sum(jnp.where(sels[r], before, 0.0), axis=-1, keepdims=True)
        route = jnp.where(lane == r, idxs[r], route)
        route = jnp.where(lane == TOP_K + r, rank, route)
        gates = jnp.where(lane == r, es[r] / den, gates)
    route_ref[...] = route.astype(I32)
    gates_ref[...] = gates
    cnt_ref[...] = jnp.broadcast_to(base_scr[0:1, :], cnt_ref.shape).astype(I32)


def _mix(hg, da_p, da_s, x_p, x_s, wo_bf16, g2, wr_pad, br_pad, tm):
    d = x_p.shape[1]
    n = x_p.shape[0] + x_s.shape[0]
    npt = x_p.shape[0] // tm
    row = lambda w: pl.BlockSpec((tm, w), lambda i: (i, 0))
    p_row = lambda w: pl.BlockSpec((tm, w), lambda i: (jnp.minimum(i, npt - 1), 0))
    s_row = lambda w: pl.BlockSpec((tm, w), lambda i: (jnp.maximum(i - npt, 0), 0))
    const = lambda shp: pl.BlockSpec(shp, lambda i: (0, 0))
    return pl.pallas_call(
        functools.partial(_mix_kernel, npt),
        grid=(n // tm,),
        in_specs=[row(HEAD_WIDTH), p_row(HEAD_WIDTH), s_row(HEAD_WIDTH), p_row(d), s_row(d),
                  const((2 * HEAD_WIDTH, d)), const((1, d)), const((d, LANES)), const((1, LANES))],
        out_specs=[row(d), row(d // 2), row(LANES), row(LANES), const((8, LANES))],
        out_shape=[jax.ShapeDtypeStruct((n, d), F32),
                   jax.ShapeDtypeStruct((n, d // 2), U32),
                   jax.ShapeDtypeStruct((n, LANES), I32),
                   jax.ShapeDtypeStruct((n, LANES), F32),
                   jax.ShapeDtypeStruct((8, LANES), I32)],
        scratch_shapes=[pltpu.VMEM((8, LANES), F32)],
        compiler_params=_cparams(("arbitrary",)),
        name="mix",
    )(hg, da_p, da_s, x_p, x_s, wo_bf16, g2, wr_pad, br_pad)


def _dispatch_copy(h_ref, xs_ref, sem, dest_ref, r, kk):
    return pltpu.make_async_copy(h_ref.at[pl.ds(r, 1)],
                                 xs_ref.at[pl.ds(dest_ref[r * TOP_K + kk], 1)], sem)


def _dispatch_kernel(zinfo_ref, dest_ref, h_ref, xs_ref, zbuf, sem, zsem):
    tm = h_ref.shape[0]
    bm = zbuf.shape[0]
    nb = xs_ref.shape[0] // bm

    def zero_copy(start):
        return pltpu.make_async_copy(zbuf, xs_ref.at[pl.ds(pl.multiple_of(start, bm), bm)], zsem)

    def for_each_zero_block(fn):
        for e in range(N_EXPERTS):
            start = zinfo_ref[e]

            @pl.when(start >= 0)
            def _():
                fn(zero_copy(start))

        def tail(b, carry):
            fn(zero_copy(b * bm))
            return carry

        lax.fori_loop(zinfo_ref[N_EXPERTS], nb, tail, 0)

    @pl.when(pl.program_id(0) == 0)
    def _():
        zbuf[...] = jnp.zeros(zbuf.shape, zbuf.dtype)
        for_each_zero_block(lambda cp: cp.start())
        for_each_zero_block(lambda cp: cp.wait())

    def issue(r, carry):
        for kk in range(TOP_K):
            _dispatch_copy(h_ref, xs_ref, sem, dest_ref, r, kk).start()
        return carry

    def drain(r, carry):
        for kk in range(TOP_K):
            _dispatch_copy(h_ref, xs_ref, sem, dest_ref, r, kk).wait()
        return carry

    lax.fori_loop(0, tm, issue, 0)
    lax.fori_loop(0, tm, drain, 0)


def _dispatch(h2p, dest_flat, zinfo, n_rows, bm, tm):
    n, w = h2p.shape
    grid_spec = pltpu.PrefetchScalarGridSpec(
        num_scalar_prefetch=1,
        grid=(n // tm,),
        in_specs=[pl.BlockSpec((tm * TOP_K,), lambda i, z: (i,), memory_space=pltpu.SMEM),
                  pl.BlockSpec((tm, w), lambda i, z: (i, 0))],
        out_specs=pl.BlockSpec(memory_space=pl.ANY),
        scratch_shapes=[pltpu.VMEM((bm, w), h2p.dtype),
                        pltpu.SemaphoreType.DMA, pltpu.SemaphoreType.DMA])
    return pl.pallas_call(
        _dispatch_kernel,
        grid_spec=grid_spec,
        out_shape=jax.ShapeDtypeStruct((n_rows, w), h2p.dtype),
        compiler_params=_cparams(("arbitrary",)),
        name="dispatch",
    )(zinfo, dest_flat, h2p)


def _moe_kernel(n_parts, be_ref, nu_ref, parts_ref, xs_ref, wg_ref, wu_ref, bg_ref, bu_ref, wd_ref,
                bd_ref, o_ref, xb_scr):
    i = pl.program_id(0)
    j = pl.program_id(1)
    parts = parts_ref[i]
    used = parts > 0
    part_rows = o_ref.shape[0] // n_parts

    @pl.when(jnp.logical_and(j == 0, used))
    def _():
        half = xs_ref.shape[1]
        lo, hi = _unpack_bf16_pairs(xs_ref[...])
        xb_scr[:, 0:half] = lo
        xb_scr[:, half:2 * half] = hi
        o_ref[...] = jnp.broadcast_to(bd_ref[0], o_ref.shape)

    @pl.when(jnp.logical_and(j == 0, jnp.logical_not(used)))
    def _():
        o_ref[...] = jnp.zeros(o_ref.shape, F32)

    def experts_on(rows):
        xb = xb_scr[0:rows, :]
        gate = _dot(xb, wg_ref[0].astype(BF16)) + bg_ref[0]
        up = _dot(xb, wu_ref[0].astype(BF16)) + bu_ref[0]
        gate = jnp.minimum(gate, SWIGLU_LIMIT)
        up = jnp.clip(up, -SWIGLU_LIMIT, SWIGLU_LIMIT)
        glu = gate * _sigmoid(SWIGLU_ALPHA * gate)
        mid = ((up + 1.0) * glu).astype(BF16)
        o_ref[0:rows, :] += _dot(mid, wd_ref[0].astype(BF16))

    for p in range(1, n_parts + 1):
        @pl.when(parts == p)
        def _(p=p):
            experts_on(p * part_rows)


def _moe(xs, block_expert, n_used, block_parts, n_parts, w_gu, b_gu, w_down, b_down, bm, th):
    rows, half = xs.shape
    d = 2 * half
    n_e, _, two_h = w_gu.shape
    hid = two_h // 2
    nj = hid // th
    nb = rows // bm
    b_gu3 = b_gu.reshape(n_e, 1, two_h)
    b_dn3 = b_down.reshape(n_e, 1, d)

    def xrow(i, j, be, nu, bp):
        return (jnp.minimum(i, jnp.maximum(nu[0] - 1, 0)), 0)

    def hcol(i, j, nu):
        return jnp.where(i < nu[0], j, nj - 1)

    grid_spec = pltpu.PrefetchScalarGridSpec(
        num_scalar_prefetch=3,
        grid=(nb, nj),
        in_specs=[pl.BlockSpec((bm, half), xrow),
                  pl.BlockSpec((1, d, th), lambda i, j, be, nu, bp: (be[i], 0, hcol(i, j, nu))),
                  pl.BlockSpec((1, d, th),
                               lambda i, j, be, nu, bp: (be[i], 0, nj + hcol(i, j, nu))),
                  pl.BlockSpec((1, 1, th), lambda i, j, be, nu, bp: (be[i], 0, hcol(i, j, nu))),
                  pl.BlockSpec((1, 1, th),
                               lambda i, j, be, nu, bp: (be[i], 0, nj + hcol(i, j, nu))),
                  pl.BlockSpec((1, th, d), lambda i, j, be, nu, bp: (be[i], hcol(i, j, nu), 0)),
                  pl.BlockSpec((1, 1, d), lambda i, j, be, nu, bp: (be[i], 0, 0))],
        out_specs=pl.BlockSpec((bm, d), lambda i, j, be, nu, bp: (i, 0)),
        scratch_shapes=[pltpu.VMEM((bm, d), BF16)])
    return pl.pallas_call(
        functools.partial(_moe_kernel, n_parts),
        grid_spec=grid_spec,
        out_shape=jax.ShapeDtypeStruct((rows, d), F32),
        compiler_params=_cparams(("arbitrary", "arbitrary")),
        name="moe",
    )(block_expert, n_used, block_parts, xs, w_gu, w_gu, b_gu3, b_gu3, w_down, b_dn3)


def _combine_kernel(n_prompt_tiles, n_tiles, dest_ref, dest_next_ref, x1_ref, gates_ref, yb_ref,
                    yp_ref, ys_ref, buf, sem):
    i = pl.program_id(0)
    tm = x1_ref.shape[0]
    slot = i % 2

    def for_each_row_copy(d_ref, s, fn):
        def body(r, carry):
            for kk in range(TOP_K):
                fn(pltpu.make_async_copy(yb_ref.at[pl.ds(d_ref[r * TOP_K + kk], 1)],
                                         buf.at[s, kk, pl.ds(r, 1)], sem.at[s]))
            return carry

        lax.fori_loop(0, tm, body, 0)

    @pl.when(i == 0)
    def _():
        for_each_row_copy(dest_ref, 0, lambda cp: cp.start())

    @pl.when(i + 1 < n_tiles)
    def _():
        for_each_row_copy(dest_next_ref, 1 - slot, lambda cp: cp.start())

    for_each_row_copy(dest_ref, slot, lambda cp: cp.wait())

    g = gates_ref[...]
    y = x1_ref[...]
    for kk in range(TOP_K):
        y = y + g[:, kk:kk + 1] * buf[slot, kk]

    @pl.when(i < n_prompt_tiles)
    def _():
        yp_ref[...] = y

    @pl.when(i >= n_prompt_tiles)
    def _():
        ys_ref[...] = y


def _combine(x1, gates, dest_flat, yb, n_prompt, tm):
    n, d = x1.shape
    npt = n_prompt // tm
    n_tiles = n // tm
    return pl.pallas_call(
        functools.partial(_combine_kernel, npt, n_tiles),
        grid=(n_tiles,),
        in_specs=[pl.BlockSpec((tm * TOP_K,), lambda i: (i,), memory_space=pltpu.SMEM),
                  pl.BlockSpec((tm * TOP_K,), lambda i: (jnp.minimum(i + 1, n_tiles - 1),),
                               memory_space=pltpu.SMEM),
                  pl.BlockSpec((tm, d), lambda i: (i, 0)),
                  pl.BlockSpec((tm, LANES), lambda i: (i, 0)),
                  pl.BlockSpec(memory_space=pl.ANY)],
        out_specs=[pl.BlockSpec((tm, d), lambda i: (jnp.minimum(i, npt - 1), 0)),
                   pl.BlockSpec((tm, d), lambda i: (jnp.maximum(i - npt, 0), 0))],
        out_shape=[jax.ShapeDtypeStruct((n_prompt, d), F32),
                   jax.ShapeDtypeStruct((n - n_prompt, d), F32)],
        scratch_shapes=[pltpu.VMEM((2, TOP_K, tm, d), F32), pltpu.SemaphoreType.DMA((2,))],
        compiler_params=_cparams(("arbitrary",)),
        name="combine",
    )(dest_flat, dest_flat, x1, gates, yb)


def _group_matrix(group):
    r = jnp.arange(256) // group
    return (r[:, None] == r[None, :]).astype(BF16)


def _tile(n, want):
    t = min(want, n)
    while n % t:
        t //= 2
    return t


def _layer(x_prompt, x_sample, cache_k, cache_v, state_hgrn, attn_norm_g, w_in, lb, hg_norm_g,
           da_q_norm_g, da_k_norm_g, da_lambda, da_subln_g, w_out, ffn_norm_g, w_router,
           b_router, w_gu, b_gu, w_down, b_down, lam_init):
    bp, s, d = x_prompt.shape
    bs, t, _ = x_sample.shape
    assert bp == 1 and t == HG_TILE and s % HG_TILE == 0
    n_p = bp * s
    n = n_p + bs * t
    x_p = x_prompt.reshape(n_p, d)
    x_s = x_sample.reshape(bs * t, d)
    row_gcd = math.gcd(n_p, bs * t)

    proj = _proj(x_p, x_s, attn_norm_g.reshape(1, d), w_in.astype(BF16), _tile(row_gcd, PROJ_TM),
                 PROJ_TN)

    hg_group = _tile(math.gcd(n_p // HG_TILE, bs), HG_GROUP)
    hg, s_all = _hgrn(proj, lb.reshape(1, HEAD_WIDTH), hg_norm_g.reshape(1, HG_DIM), state_hgrn,
                      n_p, hg_group)

    tm_da = _tile(row_gcd, DA_TM)
    gq = jnp.tile(da_q_norm_g, 2 * DA_HEADS).reshape(1, HEAD_WIDTH)
    gk = jnp.tile(da_k_norm_g, 2 * DA_HEADS).reshape(1, HEAD_WIDTH)
    qlo, qhi, kt, vx, k_p, k_s, v_p, v_s = _da_prep(proj, gq, gk, _group_matrix(DA_QK_DIM), n_p,
                                                    tm_da)
    sg = da_subln_g.reshape(1, DA_V_DIM)
    score_bound = (DA_QK_DIM ** 0.5 * LOG2_E) * jnp.max(jnp.abs(da_q_norm_g)) \
        * jnp.max(jnp.abs(da_k_norm_g))
    da_p = _attn_prompt_q(qlo, qhi, kt, vx, score_bound, da_lambda, sg, n_p, lam_init,
                          _tile(n_p, ATTN_TILE))
    past = cache_k.shape[1]
    da_s = _attn_sample(qlo, qhi, k_s, v_s, cache_k, cache_v, da_lambda, sg, n_p, lam_init)

    wr_pad = jnp.pad(w_router, ((0, 0), (0, LANES - N_EXPERTS)))
    br_pad = jnp.pad(b_router, (0, LANES - N_EXPERTS)).reshape(1, LANES)
    x1, h2p, route, gates, cnt = _mix(hg, da_p, da_s, x_p, x_s, w_out.astype(BF16),
                                      ffn_norm_g.reshape(1, d), wr_pad, br_pad,
                                      _tile(row_gcd, MIX_TM))

    bm = MOE_BM
    counts = cnt[0, :N_EXPERTS]
    padded = (counts + bm - 1) // bm * bm
    pend = jnp.cumsum(padded)
    pstart = pend - padded
    dest = (pstart[route[:, 0:TOP_K]] + route[:, TOP_K:2 * TOP_K]).reshape(-1).astype(I32)
    nb = (n * TOP_K + N_EXPERTS * (bm - 1)) // bm
    n_used = (pend[-1] // bm).astype(I32)
    blk = jnp.minimum(jnp.arange(nb, dtype=I32), jnp.maximum(n_used - 1, 0))
    block_expert = jnp.minimum(jnp.sum(pend[None, :] <= (blk * bm)[:, None], axis=1),
                               N_EXPERTS - 1).astype(I32)
    zinfo = jnp.concatenate([jnp.where(padded > 0, pend - bm, -1), n_used[None]]).astype(I32)
    n_parts = MOE_PARTS if bm % (8 * MOE_PARTS) == 0 else 1
    blk_all = jnp.arange(nb, dtype=I32)
    valid_rows = jnp.clip(counts[block_expert] - (blk_all * bm - pstart[block_expert]), 0, bm)
    part_rows = bm // n_parts
    block_parts = jnp.where(blk_all < n_used, (valid_rows + part_rows - 1) // part_rows,
                            0).astype(I32)

    xs = _dispatch(h2p, dest, zinfo, nb * bm, bm, _tile(n, DISPATCH_TM))
    yb = _moe(xs, block_expert, n_used.reshape(1), block_parts, n_parts, w_gu, b_gu, w_down,
              b_down, bm, min(MOE_TH, w_down.shape[1]))
    y_p, y_s = _combine(x1, gates, dest, yb, n_p, _tile(row_gcd, COMBINE_TM))

    return (y_p.reshape(bp, s, d), y_s.reshape(bs, t, d),
            k_p.reshape(bp, s, DA_HEADS, 2, DA_QK_DIM), v_p.reshape(bp, s, DA_HEADS, DA_V_DIM),
            s_all[0:1],
            k_s.reshape(bs, t, DA_HEADS, 2, DA_QK_DIM), v_s.reshape(bs, t, DA_HEADS, DA_V_DIM),
            s_all[hg_group:])


def kernel(x_prompt, x_sample, cache_k, cache_v, state_hgrn, attn_norm_g, w_in, hg_lb_logits,
           hg_norm_g, da_q_norm_g, da_k_norm_g, da_lambda, da_subln_g, w_out, ffn_norm_g,
           w_router, b_router, w_gu, b_gu, w_down, b_down):
    depth = w_in.shape[0]
    assert depth == 1, "single-layer step"
    lb_all = jnp.cumsum(jax.nn.softmax(hg_lb_logits.astype(F32), axis=0), axis=0)
    lam_init = 0.8 - 0.6 * math.exp(-0.3 * 0)
    outs = _layer(x_prompt, x_sample, cache_k[0], cache_v[0], state_hgrn[0], attn_norm_g[0],
                  w_in[0], lb_all[0], hg_norm_g[0], da_q_norm_g[0], da_k_norm_g[0], da_lambda[0],
                  da_subln_g[0], w_out[0], ffn_norm_g[0], w_router[0], b_router[0], w_gu[0],
                  b_gu[0], w_down[0], b_down[0], lam_init)
    y_p, y_s, k_p, v_p, s_p, k_s, v_s, s_s = outs
    return (y_p, y_s, k_p[None], v_p[None], s_p[None], k_s[None], v_s[None], s_s[None])
```

```python
import functools
import math

import jax
import jax.numpy as jnp
from jax import lax
from jax.experimental import pallas as pl
from jax.experimental.pallas import tpu as pltpu

F32 = jnp.float32
BF16 = jnp.bfloat16
I32 = jnp.int32
U32 = jnp.uint32

RMS_EPS = 1e-6
CHUNK = 64
HG_HEADS = 8
HG_DIM = 128
DA_HEADS = 8
DA_QK_DIM = 64
DA_V_DIM = 128
HEAD_WIDTH = 1024
N_EXPERTS = 32
TOP_K = 4
SWIGLU_LIMIT = 7.0
SWIGLU_ALPHA = 1.702
LANES = 128
HG_TILE = 64
HG_SUB = 16
HG_GROUP = 4
EXP_CLAMP = 80.0
LOG2_E = 1.4426950408889634
V_COLS = 2 * DA_V_DIM
ATTN_SAFE_BOUND = 50.0
VMEM_LIMIT = 60 * 1024 * 1024
PROJ_TM, PROJ_TN = 256, 3584
DA_TM = 256
ATTN_TILE = 512
MIX_TM = 256
DISPATCH_TM = 256
MOE_BM, MOE_TH = 1024, 512
MOE_PARTS = 4
COMBINE_TM = 128


def _cparams(sem, vmem=VMEM_LIMIT):
    return pltpu.CompilerParams(dimension_semantics=sem, vmem_limit_bytes=vmem)


def _dot(a, b):
    return jnp.dot(a, b, preferred_element_type=F32)


def _dot_nt(a, b):
    return lax.dot_general(a, b, (((1,), (1,)), ((), ())), preferred_element_type=F32)


def _split_bf16(x):
    hi = x.astype(BF16)
    lo = (x - hi.astype(F32)).astype(BF16)
    return hi, lo


def _sigmoid(x):
    return 1.0 / (1.0 + jnp.exp(-x))


def _group_sum(sq, gmat):
    outs = []
    for c in range(sq.shape[1] // 256):
        hi, lo = _split_bf16(sq[:, c * 256:(c + 1) * 256])
        outs.append(_dot(hi, gmat) + _dot(lo, gmat))
    return jnp.concatenate(outs, axis=1)


def _proj_kernel(n_prompt_tiles, xp_ref, xs_ref, g_ref, w_ref, o_ref):
    is_prompt = pl.program_id(1) < n_prompt_tiles
    x = jnp.where(is_prompt, xp_ref[...], xs_ref[...])
    ms = jnp.mean(x * x, axis=-1, keepdims=True)
    h = ((x * lax.rsqrt(ms + RMS_EPS)) * g_ref[...]).astype(BF16)
    o_ref[...] = _dot(h, w_ref[...])


def _proj(x_p, x_s, g, w_bf16, tm, tn):
    d = x_p.shape[1]
    n = x_p.shape[0] + x_s.shape[0]
    npt = x_p.shape[0] // tm
    cols = w_bf16.shape[1]
    return pl.pallas_call(
        functools.partial(_proj_kernel, npt),
        grid=(cols // tn, n // tm),
        in_specs=[pl.BlockSpec((tm, d), lambda j, i: (jnp.minimum(i, npt - 1), 0)),
                  pl.BlockSpec((tm, d), lambda j, i: (jnp.maximum(i - npt, 0), 0)),
                  pl.BlockSpec((1, d), lambda j, i: (0, 0)),
                  pl.BlockSpec((d, tn), lambda j, i: (0, j))],
        out_specs=pl.BlockSpec((tm, tn), lambda j, i: (i, j)),
        out_shape=jax.ShapeDtypeStruct((n, cols), F32),
        compiler_params=_cparams(("arbitrary", "arbitrary")),
        name="proj",
    )(x_p, x_s, g, w_bf16)


def _hgrn_chunk(q_raw, f_raw, v, g_raw, lb, ng, states):
    q = q_raw * _sigmoid(q_raw)
    f = lb + (1.0 - lb) * _sigmoid(f_raw)
    logf = jnp.log(f)
    k = 1.0 - f
    gate = _sigmoid(g_raw)

    row = lax.broadcasted_iota(I32, (HG_TILE, HG_TILE), 0)
    col = lax.broadcasted_iota(I32, (HG_TILE, HG_TILE), 1)
    causal = col <= row
    tri = causal.astype(BF16)
    hi, lo = _split_bf16(logf)
    b = _dot(tri, hi) + _dot(tri, lo)
    b_last = b[HG_TILE - 1:HG_TILE, :]
    q_dec = q * jnp.exp(b)
    k_dec = k * jnp.exp(b_last - b)

    outs, new_states = [], []
    for h in range(HG_HEADS):
        sl = slice(h * HG_DIM, (h + 1) * HG_DIM)
        st = states[h]
        bh = b[:, sl]
        qh = q[:, sl]
        kh = k[:, sl]
        vh = v[:, sl].astype(BF16)
        o = _dot_nt(q_dec[:, sl].astype(BF16), st.astype(BF16))
        rows = []
        for i in range(HG_TILE // HG_SUB):
            r0 = i * HG_SUB
            base = bh[r0 - 1:r0, :] if i else jnp.zeros((1, HG_DIM), F32)
            qi = qh[r0:r0 + HG_SUB, :] * jnp.exp(bh[r0:r0 + HG_SUB, :] - base)
            ki = kh * jnp.exp(jnp.minimum(base - bh, EXP_CLAMP))
            rows.append(_dot_nt(qi.astype(BF16), ki.astype(BF16)))
        scores = jnp.where(causal, jnp.concatenate(rows, axis=0), 0.0)
        o = o + _dot(scores.astype(BF16), vh)
        new_states.append(st * jnp.exp(b_last[:, sl])
                          + _dot(v[:, sl].T.astype(BF16), k_dec[:, sl].astype(BF16)))
        ms = jnp.mean(o * o, axis=-1, keepdims=True)
        outs.append((o * lax.rsqrt(ms + RMS_EPS)) * ng)
    return jnp.concatenate(outs, axis=1) * gate, new_states


def _hgrn_kernel(n_prompt_steps, group, q_ref, f_ref, i_ref, g_ref, lb_ref, ng_ref, s0_ref,
                 o_ref, sout_ref, st_scr):
    t = pl.program_id(0)
    lb = lb_ref[...]
    ng = ng_ref[...]

    def chunk(g, states):
        rows = slice(g * HG_TILE, (g + 1) * HG_TILE)
        out, states = _hgrn_chunk(q_ref[rows, :], f_ref[rows, :], i_ref[rows, :], g_ref[rows, :],
                                  lb, ng, states)
        o_ref[rows, :] = out.astype(o_ref.dtype)
        return states

    @pl.when(t == 0)
    def _():
        st_scr[...] = jnp.zeros(st_scr.shape, F32)

    @pl.when(t < n_prompt_steps)
    def _():
        states = [st_scr[h] for h in range(HG_HEADS)]
        for g in range(group):
            states = chunk(g, states)
        for h in range(HG_HEADS):
            st_scr[h] = states[h]

    @pl.when(t == n_prompt_steps - 1)
    def _():
        for h in range(HG_HEADS):
            sout_ref[0, h] = st_scr[h].T
        if group > 1:
            sout_ref[1:group] = jnp.zeros((group - 1,) + tuple(sout_ref.shape[1:]), F32)

    @pl.when(t >= n_prompt_steps)
    def _():
        for g in range(group):
            states = chunk(g, [s0_ref[g, h].T for h in range(HG_HEADS)])
            for h in range(HG_HEADS):
                sout_ref[g, h] = states[h].T


def _hgrn(proj, lb, norm_g, state_s, n_prompt, group):
    n = proj.shape[0]
    rows = group * HG_TILE
    n_prompt_steps = n_prompt // rows
    s0_all = jnp.concatenate([jnp.zeros((group,) + state_s.shape[1:], F32), state_s], axis=0)
    seq = lambda t: (jnp.maximum(t - (n_prompt_steps - 1), 0), 0, 0, 0)
    col_spec = lambda c: pl.BlockSpec((rows, HEAD_WIDTH), lambda t, c=c: (t, c))
    return pl.pallas_call(
        functools.partial(_hgrn_kernel, n_prompt_steps, group),
        grid=(n // rows,),
        in_specs=[col_spec(0), col_spec(1), col_spec(2), col_spec(3),
                  pl.BlockSpec((1, HEAD_WIDTH), lambda t: (0, 0)),
                  pl.BlockSpec((1, HG_DIM), lambda t: (0, 0)),
                  pl.BlockSpec((group, HG_HEADS, HG_DIM, HG_DIM), seq)],
        out_specs=[pl.BlockSpec((rows, HEAD_WIDTH), lambda t: (t, 0)),
                   pl.BlockSpec((group, HG_HEADS, HG_DIM, HG_DIM), seq)],
        out_shape=[jax.ShapeDtypeStruct((n, HEAD_WIDTH), BF16),
                   jax.ShapeDtypeStruct(s0_all.shape, F32)],
        scratch_shapes=[pltpu.VMEM((HG_HEADS, HG_DIM, HG_DIM), F32)],
        compiler_params=_cparams(("arbitrary",)),
        name="hgrn",
    )(proj, proj, proj, proj, lb, norm_g, s0_all)


def _da_prep_kernel(n_prompt_tiles, dq_ref, dk_ref, dv_ref, gq_ref, gk_ref, gmat_ref,
                    qlo_ref, qhi_ref, kt_ref, vx_ref, kp_ref, ks_ref, vp_ref, vs_ref):
    i = pl.program_id(0)
    gmat = gmat_ref[...]
    inv = 1.0 / DA_QK_DIM

    dq = dq_ref[...]
    qn = dq * lax.rsqrt(_group_sum(dq * dq, gmat) * inv + RMS_EPS) * gq_ref[...]
    qn = qn * (DA_QK_DIM ** -0.5 * LOG2_E)
    lane = lax.broadcasted_iota(I32, qn.shape, 1)
    first = (lane & DA_QK_DIM) == 0
    qlo_ref[...] = jnp.where(first, qn, 0.0).astype(BF16)
    qhi_ref[...] = jnp.where(first, 0.0, qn).astype(BF16)

    dk = dk_ref[...]
    kn = dk * lax.rsqrt(_group_sum(dk * dk, gmat) * inv + RMS_EPS) * gk_ref[...]
    v = dv_ref[...]
    kt_ref[...] = kn.T.astype(BF16)
    ones_col = (lax.broadcasted_iota(I32, (v.shape[0], V_COLS - DA_V_DIM), 1) == 0).astype(F32)
    cols = []
    for h in range(DA_HEADS):
        cols += [v[:, h * DA_V_DIM:(h + 1) * DA_V_DIM], ones_col]
    vx_ref[...] = jnp.concatenate(cols, axis=1).astype(BF16)

    @pl.when(i < n_prompt_tiles)
    def _():
        kp_ref[...] = kn
        vp_ref[...] = v

    @pl.when(i >= n_prompt_tiles)
    def _():
        ks_ref[...] = kn
        vs_ref[...] = v


def _da_prep(proj, gq, gk, gmat, n_prompt, tm):
    n = proj.shape[0]
    npt = n_prompt // tm
    full = lambda: pl.BlockSpec((tm, HEAD_WIDTH), lambda i: (i, 0))
    p_spec = lambda: pl.BlockSpec((tm, HEAD_WIDTH), lambda i: (jnp.minimum(i, npt - 1), 0))
    s_spec = lambda: pl.BlockSpec((tm, HEAD_WIDTH), lambda i: (jnp.maximum(i - npt, 0), 0))
    vec = lambda: pl.BlockSpec((1, HEAD_WIDTH), lambda i: (0, 0))
    return pl.pallas_call(
        functools.partial(_da_prep_kernel, npt),
        grid=(n // tm,),
        in_specs=[pl.BlockSpec((tm, HEAD_WIDTH), lambda i: (i, 4)),
                  pl.BlockSpec((tm, HEAD_WIDTH), lambda i: (i, 5)),
                  pl.BlockSpec((tm, HEAD_WIDTH), lambda i: (i, 6)),
                  vec(), vec(),
                  pl.BlockSpec((256, 256), lambda i: (0, 0))],
        out_specs=[full(), full(),
                   pl.BlockSpec((HEAD_WIDTH, tm), lambda i: (0, i)),
                   pl.BlockSpec((tm, DA_HEADS * V_COLS), lambda i: (i, 0)),
                   p_spec(), s_spec(), p_spec(), s_spec()],
        out_shape=[jax.ShapeDtypeStruct((n, HEAD_WIDTH), BF16),
                   jax.ShapeDtypeStruct((n, HEAD_WIDTH), BF16),
                   jax.ShapeDtypeStruct((HEAD_WIDTH, n), BF16),
                   jax.ShapeDtypeStruct((n, DA_HEADS * V_COLS), BF16),
                   jax.ShapeDtypeStruct((n_prompt, HEAD_WIDTH), F32),
                   jax.ShapeDtypeStruct((n - n_prompt, HEAD_WIDTH), F32),
                   jax.ShapeDtypeStruct((n_prompt, HEAD_WIDTH), F32),
                   jax.ShapeDtypeStruct((n - n_prompt, HEAD_WIDTH), F32)],
        compiler_params=_cparams(("arbitrary",)),
        name="da_prep",
    )(proj, proj, proj, gq, gk, gmat)


def _lambda_of(lp, lam_init):
    a = jnp.sum(lp[0:1, :] * lp[1:2, :], axis=-1, keepdims=True)
    b = jnp.sum(lp[2:3, :] * lp[3:4, :], axis=-1, keepdims=True)
    return jnp.exp(a) - jnp.exp(b) + lam_init


def _subln(o, g, lam_init):
    ms = jnp.mean(o * o, axis=-1, keepdims=True)
    return (o * lax.rsqrt(ms + RMS_EPS)) * g * (1.0 - lam_init)


def _attn_q_kernel(lam_init, tile, qi_ref, kj_ref, bounded_ref, qlo_ref, qhi_ref, kt_ref, vx_ref,
                   bound_ref, lp_ref, sg_ref, o_ref, m_scr, acc_scr):
    s = pl.program_id(0)
    qi = qi_ref[s]
    kj = kj_ref[s]
    bounded = bounded_ref[0] != 0
    n_maps = 2 * DA_HEADS

    @pl.when(kj == 0)
    def _():
        m_scr[...] = jnp.full(m_scr.shape, -jnp.inf, F32)
        acc_scr[...] = jnp.zeros(acc_scr.shape, F32)

    def scores(r):
        h, c = divmod(r, 2)
        sl = slice(h * DA_V_DIM, (h + 1) * DA_V_DIM)
        return _dot((qlo_ref, qhi_ref)[c][:, sl], kt_ref[sl, :])

    def values(r):
        h = r // 2
        return vx_ref[:, h * V_COLS:(h + 1) * V_COLS]

    def sweep_bounded(visible):
        bound = bound_ref[0:1, 0:1]
        for r in range(n_maps):
            e = jnp.exp2(scores(r) - bound)
            if visible is not None:
                e = jnp.where(visible, e, 0.0)
            acc_scr[r] += _dot(e.astype(BF16), values(r))

    def sweep_online(visible):
        for r in range(n_maps):
            sc = scores(r)
            if visible is not None:
                sc = jnp.where(visible, sc, -jnp.inf)
            m_prev = m_scr[r]
            m_new = jnp.maximum(m_prev, jnp.max(sc, axis=-1, keepdims=True))
            alpha = jnp.exp2(m_prev - m_new)
            e = jnp.exp2(sc - m_new)
            acc_scr[r] = alpha * acc_scr[r] + _dot(e.astype(BF16), values(r))
            m_scr[r] = m_new

    def chunk_mask():
        shift = CHUNK.bit_length() - 1
        qry_chunk = lax.broadcasted_iota(I32, (tile, tile), 0) >> shift
        key_chunk = lax.broadcasted_iota(I32, (tile, tile), 1) >> shift
        return key_chunk <= qry_chunk

    off_diag = kj < qi
    on_diag = kj == qi
    unbounded = jnp.logical_not(bounded)

    @pl.when(jnp.logical_and(off_diag, bounded))
    def _():
        sweep_bounded(None)

    @pl.when(jnp.logical_and(on_diag, bounded))
    def _():
        sweep_bounded(chunk_mask())

    @pl.when(jnp.logical_and(off_diag, unbounded))
    def _():
        sweep_online(None)

    @pl.when(jnp.logical_and(on_diag, unbounded))
    def _():
        sweep_online(chunk_mask())

    @pl.when(on_diag)
    def _():
        lam = _lambda_of(lp_ref[...], lam_init)
        sg = sg_ref[...]
        outs = []
        for h in range(DA_HEADS):
            acc0 = acc_scr[2 * h]
            acc1 = acc_scr[2 * h + 1]
            a0 = acc0[:, 0:DA_V_DIM] / acc0[:, DA_V_DIM:DA_V_DIM + 1]
            a1 = acc1[:, 0:DA_V_DIM] / acc1[:, DA_V_DIM:DA_V_DIM + 1]
            outs.append(_subln(a0 - lam * a1, sg, lam_init))
        o_ref[...] = jnp.concatenate(outs, axis=1).astype(o_ref.dtype)


def _attn_prompt_q(qlo, qhi, kt, vx, bound, lp, sg, n_prompt, lam_init, tile):
    nq = n_prompt // tile
    qi_l, kj_l = [], []
    for i in range(nq):
        for j in range(i + 1):
            qi_l.append(i)
            kj_l.append(j)
    qi = jnp.asarray(qi_l, I32)
    kj = jnp.asarray(kj_l, I32)
    bounded = (bound <= ATTN_SAFE_BOUND).astype(I32).reshape(1)
    bound_row = jnp.broadcast_to(bound.astype(F32), (1, LANES))
    const = lambda shp: pl.BlockSpec(shp, lambda s, qi, kj, bd: (0, 0))
    grid_spec = pltpu.PrefetchScalarGridSpec(
        num_scalar_prefetch=3,
        grid=(len(qi_l),),
        in_specs=[pl.BlockSpec((tile, HEAD_WIDTH), lambda s, qi, kj, bd: (qi[s], 0)),
                  pl.BlockSpec((tile, HEAD_WIDTH), lambda s, qi, kj, bd: (qi[s], 0)),
                  pl.BlockSpec((HEAD_WIDTH, tile), lambda s, qi, kj, bd: (0, kj[s])),
                  pl.BlockSpec((tile, DA_HEADS * V_COLS), lambda s, qi, kj, bd: (kj[s], 0)),
                  const((1, LANES)), const((4, DA_QK_DIM)), const((1, DA_V_DIM))],
        out_specs=pl.BlockSpec((tile, HEAD_WIDTH), lambda s, qi, kj, bd: (qi[s], 0)),
        scratch_shapes=[pltpu.VMEM((2 * DA_HEADS, tile, 1), F32),
                        pltpu.VMEM((2 * DA_HEADS, tile, V_COLS), F32)])
    return pl.pallas_call(
        functools.partial(_attn_q_kernel, lam_init, tile),
        grid_spec=grid_spec,
        out_shape=jax.ShapeDtypeStruct((n_prompt, HEAD_WIDTH), BF16),
        compiler_params=_cparams(("arbitrary",)),
        name="attn_q",
    )(qi, kj, bounded, qlo, qhi, kt, vx, bound_row, lp, sg)


def _attn_s_kernel(lam_init, qlo_ref, qhi_ref, kn_ref, vn_ref, ckt_ref, cv_ref, lp_ref, sg_ref,
                   o_ref):
    past = ckt_ref.shape[-1]
    lam = _lambda_of(lp_ref[...], lam_init)
    sg = sg_ref[...]
    outs = []
    for h in range(DA_HEADS):
        sl = slice(h * DA_V_DIM, (h + 1) * DA_V_DIM)
        kn = kn_ref[:, sl].astype(BF16)
        vc = cv_ref[0, pl.ds(h, past, stride=DA_HEADS), :].astype(BF16)
        vn = vn_ref[:, sl].astype(BF16)
        acc = None
        for c, q_ref in enumerate((qlo_ref, qhi_ref)):
            qh = q_ref[:, sl]
            kct = ckt_ref[0, h, c].astype(BF16)
            s_c = _dot(qh[:, c * DA_QK_DIM:(c + 1) * DA_QK_DIM], kct)
            s_n = _dot_nt(qh, kn)
            m = jnp.maximum(jnp.max(s_c, axis=-1, keepdims=True),
                            jnp.max(s_n, axis=-1, keepdims=True))
            e_c = jnp.exp2(s_c - m)
            e_n = jnp.exp2(s_n - m)
            l = jnp.sum(e_c, axis=-1, keepdims=True) + jnp.sum(e_n, axis=-1, keepdims=True)
            part = (_dot(e_c.astype(BF16), vc) + _dot(e_n.astype(BF16), vn)) / l
            acc = part if c == 0 else acc - lam * part
        outs.append(_subln(acc, sg, lam_init))
    o_ref[...] = jnp.concatenate(outs, axis=1).astype(o_ref.dtype)


def _attn_sample(qlo, qhi, ks, vs, cache_k, cache_v, lp, sg, n_prompt, lam_init):
    bs, past = cache_k.shape[0], cache_k.shape[1]
    cache_kt = jnp.transpose(cache_k, (0, 2, 3, 4, 1))
    cache_v = cache_v.reshape(bs, past * DA_HEADS, DA_V_DIM)
    t = ks.shape[0] // bs
    off = n_prompt // t
    return pl.pallas_call(
        functools.partial(_attn_s_kernel, lam_init),
        grid=(bs,),
        in_specs=[pl.BlockSpec((t, HEAD_WIDTH), lambda b: (off + b, 0)),
                  pl.BlockSpec((t, HEAD_WIDTH), lambda b: (off + b, 0)),
                  pl.BlockSpec((t, HEAD_WIDTH), lambda b: (b, 0)),
                  pl.BlockSpec((t, HEAD_WIDTH), lambda b: (b, 0)),
                  pl.BlockSpec((1, DA_HEADS, 2, DA_QK_DIM, past), lambda b: (b, 0, 0, 0, 0)),
                  pl.BlockSpec((1, past * DA_HEADS, DA_V_DIM), lambda b: (b, 0, 0)),
                  pl.BlockSpec((4, DA_QK_DIM), lambda b: (0, 0)),
                  pl.BlockSpec((1, DA_V_DIM), lambda b: (0, 0))],
        out_specs=pl.BlockSpec((t, HEAD_WIDTH), lambda b: (b, 0)),
        out_shape=jax.ShapeDtypeStruct((bs * t, HEAD_WIDTH), BF16),
        compiler_params=_cparams(("arbitrary",)),
        name="attn_s",
    )(qlo, qhi, ks, vs, cache_kt, cache_v, lp, sg)


def _pack_bf16_pairs(x):
    half = x.shape[1] // 2
    lo = lax.bitcast_convert_type(x[:, :half].astype(BF16).astype(F32), U32)
    hi = lax.bitcast_convert_type(x[:, half:].astype(BF16).astype(F32), U32)
    return (lo >> 16) | (hi & jnp.uint32(0xFFFF0000))


def _unpack_bf16_pairs(w):
    lo = lax.bitcast_convert_type(w << 16, F32).astype(BF16)
    hi = lax.bitcast_convert_type(w & jnp.uint32(0xFFFF0000), F32).astype(BF16)
    return lo, hi


def _mix_kernel(n_prompt_tiles, hg_ref, dap_ref, das_ref, xp_ref, xs_ref, wo_ref, g2_ref, wr_ref,
                br_ref, x1_ref, h2p_ref, route_ref, gates_ref, cnt_ref, base_scr):
    i = pl.program_id(0)

    @pl.when(i == 0)
    def _():
        base_scr[...] = jnp.zeros(base_scr.shape, F32)

    tm = xp_ref.shape[0]
    is_prompt = i < n_prompt_tiles
    da = jnp.where(is_prompt, dap_ref[...], das_ref[...])
    x = jnp.where(is_prompt, xp_ref[...], xs_ref[...])
    x1 = x + _dot(hg_ref[...], wo_ref[0:HEAD_WIDTH, :]) \
        + _dot(da, wo_ref[HEAD_WIDTH:2 * HEAD_WIDTH, :])
    x1_ref[...] = x1
    ms = jnp.mean(x1 * x1, axis=-1, keepdims=True)
    h2 = (x1 * lax.rsqrt(ms + RMS_EPS)) * g2_ref[...]
    h2p_ref[...] = _pack_bf16_pairs(h2)

    h_hi, h_lo = _split_bf16(h2)
    w_hi, w_lo = _split_bf16(wr_ref[...])
    logits = _dot(h_hi, w_hi) + _dot(h_lo, w_hi) + _dot(h_hi, w_lo) + br_ref[...]
    lane = lax.broadcasted_iota(I32, logits.shape, 1)
    lanef = lane.astype(F32)
    lg = jnp.where(lane < N_EXPERTS, logits, -jnp.inf)

    sels, vals, idxs = [], [], []
    for _ in range(TOP_K):
        mx = jnp.max(lg, axis=-1, keepdims=True)
        idx = jnp.min(jnp.where(lg == mx, lanef, float(LANES)), axis=-1, keepdims=True)
        sel = lanef == idx
        lg = jnp.where(sel, -jnp.inf, lg)
        sels.append(sel)
        vals.append(mx)
        idxs.append(idx)
    es = [jnp.exp(v - vals[0]) for v in vals]
    den = es[0] + es[1] + es[2] + es[3]

    onehot = jnp.zeros(logits.shape, F32)
    for sel in sels:
        onehot = onehot + sel.astype(F32)
    row = lax.broadcasted_iota(I32, (tm, tm), 0)
    col = lax.broadcasted_iota(I32, (tm, tm), 1)
    strict = (col < row).astype(BF16)
    before = _dot(strict, onehot.astype(BF16)) + base_scr[0:1, :]
    base_scr[0:1, :] = base_scr[0:1, :] + jnp.sum(onehot, axis=0, keepdims=True)

    route = jnp.zeros(logits.shape, F32)
    gates = jnp.zeros(logits.shape, F32)
    for r in range(TOP_K):
        rank = jnp.sum(jnp.where(sels[r], before, 0.0), axis=-1, keepdims=True)
        route = jnp.where(lane == r, idxs[r], route)
        route = jnp.where(lane == TOP_K + r, rank, route)
        gates = jnp.where(lane == r, es[r] / den, gates)
    route_ref[...] = route.astype(I32)
    gates_ref[...] = gates
    cnt_ref[...] = jnp.broadcast_to(base_scr[0:1, :], cnt_ref.shape).astype(I32)


def _mix(hg, da_p, da_s, x_p, x_s, wo_bf16, g2, wr_pad, br_pad, tm):
    d = x_p.shape[1]
    n = x_p.shape[0] + x_s.shape[0]
    npt = x_p.shape[0] // tm
    row = lambda w: pl.BlockSpec((tm, w), lambda i: (i, 0))
    p_row = lambda w: pl.BlockSpec((tm, w), lambda i: (jnp.minimum(i, npt - 1), 0))
    s_row = lambda w: pl.BlockSpec((tm, w), lambda i: (jnp.maximum(i - npt, 0), 0))
    const = lambda shp: pl.BlockSpec(shp, lambda i: (0, 0))
    return pl.pallas_call(
        functools.partial(_mix_kernel, npt),
        grid=(n // tm,),
        in_specs=[row(HEAD_WIDTH), p_row(HEAD_WIDTH), s_row(HEAD_WIDTH), p_row(d), s_row(d),
                  const((2 * HEAD_WIDTH, d)), const((1, d)), const((d, LANES)), const((1, LANES))],
        out_specs=[row(d), row(d // 2), row(LANES), row(LANES), const((8, LANES))],
        out_shape=[jax.ShapeDtypeStruct((n, d), F32),
                   jax.ShapeDtypeStruct((n, d // 2), U32),
                   jax.ShapeDtypeStruct((n, LANES), I32),
                   jax.ShapeDtypeStruct((n, LANES), F32),
                   jax.ShapeDtypeStruct((8, LANES), I32)],
        scratch_shapes=[pltpu.VMEM((8, LANES), F32)],
        compiler_params=_cparams(("arbitrary",)),
        name="mix",
    )(hg, da_p, da_s, x_p, x_s, wo_bf16, g2, wr_pad, br_pad)


def _dispatch_copy(h_ref, xs_ref, sem, dest_ref, r, kk):
    return pltpu.make_async_copy(h_ref.at[pl.ds(r, 1)],
                                 xs_ref.at[pl.ds(dest_ref[r * TOP_K + kk], 1)], sem)


def _dispatch_kernel(zinfo_ref, dest_ref, h_ref, xs_ref, zbuf, sem, zsem):
    tm = h_ref.shape[0]
    bm = zbuf.shape[0]
    nb = xs_ref.shape[0] // bm

    def zero_copy(start):
        return pltpu.make_async_copy(zbuf, xs_ref.at[pl.ds(pl.multiple_of(start, bm), bm)], zsem)

    def for_each_zero_block(fn):
        for e in range(N_EXPERTS):
            start = zinfo_ref[e]

            @pl.when(start >= 0)
            def _():
                fn(zero_copy(start))

        def tail(b, carry):
            fn(zero_copy(b * bm))
            return carry

        lax.fori_loop(zinfo_ref[N_EXPERTS], nb, tail, 0)

    @pl.when(pl.program_id(0) == 0)
    def _():
        zbuf[...] = jnp.zeros(zbuf.shape, zbuf.dtype)
        for_each_zero_block(lambda cp: cp.start())
        for_each_zero_block(lambda cp: cp.wait())

    def issue(r, carry):
        for kk in range(TOP_K):
            _dispatch_copy(h_ref, xs_ref, sem, dest_ref, r, kk).start()
        return carry

    def drain(r, carry):
        for kk in range(TOP_K):
            _dispatch_copy(h_ref, xs_ref, sem, dest_ref, r, kk).wait()
        return carry

    lax.fori_loop(0, tm, issue, 0)
    lax.fori_loop(0, tm, drain, 0)


def _dispatch(h2p, dest_flat, zinfo, n_rows, bm, tm):
    n, w = h2p.shape
    grid_spec = pltpu.PrefetchScalarGridSpec(
        num_scalar_prefetch=1,
        grid=(n // tm,),
        in_specs=[pl.BlockSpec((tm * TOP_K,), lambda i, z: (i,), memory_space=pltpu.SMEM),
                  pl.BlockSpec((tm, w), lambda i, z: (i, 0))],
        out_specs=pl.BlockSpec(memory_space=pl.ANY),
        scratch_shapes=[pltpu.VMEM((bm, w), h2p.dtype),
                        pltpu.SemaphoreType.DMA, pltpu.SemaphoreType.DMA])
    return pl.pallas_call(
        _dispatch_kernel,
        grid_spec=grid_spec,
        out_shape=jax.ShapeDtypeStruct((n_rows, w), h2p.dtype),
        compiler_params=_cparams(("arbitrary",)),
        name="dispatch",
    )(zinfo, dest_flat, h2p)


def _moe_kernel(n_parts, be_ref, nu_ref, parts_ref, xs_ref, wg_ref, wu_ref, bg_ref, bu_ref, wd_ref,
                bd_ref, o_ref, xb_scr):
    i = pl.program_id(0)
    j = pl.program_id(1)
    parts = parts_ref[i]
    used = parts > 0
    part_rows = o_ref.shape[0] // n_parts

    @pl.when(jnp.logical_and(j == 0, used))
    def _():
        half = xs_ref.shape[1]
        lo, hi = _unpack_bf16_pairs(xs_ref[...])
        xb_scr[:, 0:half] = lo
        xb_scr[:, half:2 * half] = hi
        o_ref[...] = jnp.broadcast_to(bd_ref[0], o_ref.shape)

    @pl.when(jnp.logical_and(j == 0, jnp.logical_not(used)))
    def _():
        o_ref[...] = jnp.zeros(o_ref.shape, F32)

    def experts_on(rows):
        xb = xb_scr[0:rows, :]
        gate = _dot(xb, wg_ref[0].astype(BF16)) + bg_ref[0]
        up = _dot(xb, wu_ref[0].astype(BF16)) + bu_ref[0]
        gate = jnp.minimum(gate, SWIGLU_LIMIT)
        up = jnp.clip(up, -SWIGLU_LIMIT, SWIGLU_LIMIT)
        glu = gate * _sigmoid(SWIGLU_ALPHA * gate)
        mid = ((up + 1.0) * glu).astype(BF16)
        o_ref[0:rows, :] += _dot(mid, wd_ref[0].astype(BF16))

    for p in range(1, n_parts + 1):
        @pl.when(parts == p)
        def _(p=p):
            experts_on(p * part_rows)


def _moe(xs, block_expert, n_used, block_parts, n_parts, w_gu, b_gu, w_down, b_down, bm, th):
    rows, half = xs.shape
    d = 2 * half
    n_e, _, two_h = w_gu.shape
    hid = two_h // 2
    nj = hid // th
    nb = rows // bm
    b_gu3 = b_gu.reshape(n_e, 1, two_h)
    b_dn3 = b_down.reshape(n_e, 1, d)

    def xrow(i, j, be, nu, bp):
        return (jnp.minimum(i, jnp.maximum(nu[0] - 1, 0)), 0)

    def hcol(i, j, nu):
        return jnp.where(i < nu[0], j, nj - 1)

    grid_spec = pltpu.PrefetchScalarGridSpec(
        num_scalar_prefetch=3,
        grid=(nb, nj),
        in_specs=[pl.BlockSpec((bm, half), xrow),
                  pl.BlockSpec((1, d, th), lambda i, j, be, nu, bp: (be[i], 0, hcol(i, j, nu))),
                  pl.BlockSpec((1, d, th),
                               lambda i, j, be, nu, bp: (be[i], 0, nj + hcol(i, j, nu))),
                  pl.BlockSpec((1, 1, th), lambda i, j, be, nu, bp: (be[i], 0, hcol(i, j, nu))),
                  pl.BlockSpec((1, 1, th),
                               lambda i, j, be, nu, bp: (be[i], 0, nj + hcol(i, j, nu))),
                  pl.BlockSpec((1, th, d), lambda i, j, be, nu, bp: (be[i], hcol(i, j, nu), 0)),
                  pl.BlockSpec((1, 1, d), lambda i, j, be, nu, bp: (be[i], 0, 0))],
        out_specs=pl.BlockSpec((bm, d), lambda i, j, be, nu, bp: (i, 0)),
        scratch_shapes=[pltpu.VMEM((bm, d), BF16)])
    return pl.pallas_call(
        functools.partial(_moe_kernel, n_parts),
        grid_spec=grid_spec,
        out_shape=jax.ShapeDtypeStruct((rows, d), F32),
        compiler_params=_cparams(("arbitrary", "arbitrary")),
        name="moe",
    )(block_expert, n_used, block_parts, xs, w_gu, w_gu, b_gu3, b_gu3, w_down, b_dn3)


def _combine_kernel(n_prompt_tiles, n_tiles, dest_ref, dest_next_ref, x1_ref, gates_ref, yb_ref,
                    yp_ref, ys_ref, buf, sem):
    i = pl.program_id(0)
    tm = x1_ref.shape[0]
    slot = i % 2

    def for_each_row_copy(d_ref, s, fn):
        def body(r, carry):
            for kk in range(TOP_K):
                fn(pltpu.make_async_copy(yb_ref.at[pl.ds(d_ref[r * TOP_K + kk], 1)],
                                         buf.at[s, kk, pl.ds(r, 1)], sem.at[s]))
            return carry

        lax.fori_loop(0, tm, body, 0)

    @pl.when(i == 0)
    def _():
        for_each_row_copy(dest_ref, 0, lambda cp: cp.start())

    @pl.when(i + 1 < n_tiles)
    def _():
        for_each_row_copy(dest_next_ref, 1 - slot, lambda cp: cp.start())

    for_each_row_copy(dest_ref, slot, lambda cp: cp.wait())

    g = gates_ref[...]
    y = x1_ref[...]
    for kk in range(TOP_K):
        y = y + g[:, kk:kk + 1] * buf[slot, kk]

    @pl.when(i < n_prompt_tiles)
    def _():
        yp_ref[...] = y

    @pl.when(i >= n_prompt_tiles)
    def _():
        ys_ref[...] = y


def _combine(x1, gates, dest_flat, yb, n_prompt, tm):
    n, d = x1.shape
    npt = n_prompt // tm
    n_tiles = n // tm
    return pl.pallas_call(
        functools.partial(_combine_kernel, npt, n_tiles),
        grid=(n_tiles,),
        in_specs=[pl.BlockSpec((tm * TOP_K,), lambda i: (i,), memory_space=pltpu.SMEM),
                  pl.BlockSpec((tm * TOP_K,), lambda i: (jnp.minimum(i + 1, n_tiles - 1),),
                               memory_space=pltpu.SMEM),
                  pl.BlockSpec((tm, d), lambda i: (i, 0)),
                  pl.BlockSpec((tm, LANES), lambda i: (i, 0)),
                  pl.BlockSpec(memory_space=pl.ANY)],
        out_specs=[pl.BlockSpec((tm, d), lambda i: (jnp.minimum(i, npt - 1), 0)),
                   pl.BlockSpec((tm, d), lambda i: (jnp.maximum(i - npt, 0), 0))],
        out_shape=[jax.ShapeDtypeStruct((n_prompt, d), F32),
                   jax.ShapeDtypeStruct((n - n_prompt, d), F32)],
        scratch_shapes=[pltpu.VMEM((2, TOP_K, tm, d), F32), pltpu.SemaphoreType.DMA((2,))],
        compiler_params=_cparams(("arbitrary",)),
        name="combine",
    )(dest_flat, dest_flat, x1, gates, yb)


def _group_matrix(group):
    r = jnp.arange(256) // group
    return (r[:, None] == r[None, :]).astype(BF16)


def _tile(n, want):
    t = min(want, n)
    while n % t:
        t //= 2
    return t


def _layer(x_prompt, x_sample, cache_k, cache_v, state_hgrn, attn_norm_g, w_in, lb, hg_norm_g,
           da_q_norm_g, da_k_norm_g, da_lambda, da_subln_g, w_out, ffn_norm_g, w_router,
           b_router, w_gu, b_gu, w_down, b_down, lam_init):
    bp, s, d = x_prompt.shape
    bs, t, _ = x_sample.shape
    assert bp == 1 and t == HG_TILE and s % HG_TILE == 0
    n_p = bp * s
    n = n_p + bs * t
    x_p = x_prompt.reshape(n_p, d)
    x_s = x_sample.reshape(bs * t, d)
    row_gcd = math.gcd(n_p, bs * t)

    proj = _proj(x_p, x_s, attn_norm_g.reshape(1, d), w_in.astype(BF16), _tile(row_gcd, PROJ_TM),
                 PROJ_TN)

    hg_group = _tile(math.gcd(n_p // HG_TILE, bs), HG_GROUP)
    hg, s_all = _hgrn(proj, lb.reshape(1, HEAD_WIDTH), hg_norm_g.reshape(1, HG_DIM), state_hgrn,
                      n_p, hg_group)

    tm_da = _tile(row_gcd, DA_TM)
    gq = jnp.tile(da_q_norm_g, 2 * DA_HEADS).reshape(1, HEAD_WIDTH)
    gk = jnp.tile(da_k_norm_g, 2 * DA_HEADS).reshape(1, HEAD_WIDTH)
    qlo, qhi, kt, vx, k_p, k_s, v_p, v_s = _da_prep(proj, gq, gk, _group_matrix(DA_QK_DIM), n_p,
                                                    tm_da)
    sg = da_subln_g.reshape(1, DA_V_DIM)
    score_bound = (DA_QK_DIM ** 0.5 * LOG2_E) * jnp.max(jnp.abs(da_q_norm_g)) \
        * jnp.max(jnp.abs(da_k_norm_g))
    da_p = _attn_prompt_q(qlo, qhi, kt, vx, score_bound, da_lambda, sg, n_p, lam_init,
                          _tile(n_p, ATTN_TILE))
    past = cache_k.shape[1]
    da_s = _attn_sample(qlo, qhi, k_s, v_s, cache_k, cache_v, da_lambda, sg, n_p, lam_init)

    wr_pad = jnp.pad(w_router, ((0, 0), (0, LANES - N_EXPERTS)))
    br_pad = jnp.pad(b_router, (0, LANES - N_EXPERTS)).reshape(1, LANES)
    x1, h2p, route, gates, cnt = _mix(hg, da_p, da_s, x_p, x_s, w_out.astype(BF16),
                                      ffn_norm_g.reshape(1, d), wr_pad, br_pad,
                                      _tile(row_gcd, MIX_TM))

    bm = MOE_BM
    counts = cnt[0, :N_EXPERTS]
    padded = (counts + bm - 1) // bm * bm
    pend = jnp.cumsum(padded)
    pstart = pend - padded
    dest = (pstart[route[:, 0:TOP_K]] + route[:, TOP_K:2 * TOP_K]).reshape(-1).astype(I32)
    nb = (n * TOP_K + N_EXPERTS * (bm - 1)) // bm
    n_used = (pend[-1] // bm).astype(I32)
    blk = jnp.minimum(jnp.arange(nb, dtype=I32), jnp.maximum(n_used - 1, 0))
    block_expert = jnp.minimum(jnp.sum(pend[None, :] <= (blk * bm)[:, None], axis=1),
                               N_EXPERTS - 1).astype(I32)
    zinfo = jnp.concatenate([jnp.where(padded > 0, pend - bm, -1), n_used[None]]).astype(I32)
    n_parts = MOE_PARTS if bm % (8 * MOE_PARTS) == 0 else 1
    blk_all = jnp.arange(nb, dtype=I32)
    valid_rows = jnp.clip(counts[block_expert] - (blk_all * bm - pstart[block_expert]), 0, bm)
    part_rows = bm // n_parts
    block_parts = jnp.where(blk_all < n_used, (valid_rows + part_rows - 1) // part_rows,
                            0).astype(I32)

    xs = _dispatch(h2p, dest, zinfo, nb * bm, bm, _tile(n, DISPATCH_TM))
    yb = _moe(xs, block_expert, n_used.reshape(1), block_parts, n_parts, w_gu, b_gu, w_down,
              b_down, bm, min(MOE_TH, w_down.shape[1]))
    y_p, y_s = _combine(x1, gates, dest, yb, n_p, _tile(row_gcd, COMBINE_TM))

    return (y_p.reshape(bp, s, d), y_s.reshape(bs, t, d),
            k_p.reshape(bp, s, DA_HEADS, 2, DA_QK_DIM), v_p.reshape(bp, s, DA_HEADS, DA_V_DIM),
            s_all[0:1],
            k_s.reshape(bs, t, DA_HEADS, 2, DA_QK_DIM), v_s.reshape(bs, t, DA_HEADS, DA_V_DIM),
            s_all[hg_group:])


def kernel(x_prompt, x_sample, cache_k, cache_v, state_hgrn, attn_norm_g, w_in, hg_lb_logits,
           hg_norm_g, da_q_norm_g, da_k_norm_g, da_lambda, da_subln_g, w_out, ffn_norm_g,
           w_router, b_router, w_gu, b_gu, w_down, b_down):
    depth = w_in.shape[0]
    assert depth == 1, "single-layer step"
    lb_all = jnp.cumsum(jax.nn.softmax(hg_lb_logits.astype(F32), axis=0), axis=0)
    lam_init = 0.8 - 0.6 * math.exp(-0.3 * 0)
    outs = _layer(x_prompt, x_sample, cache_k[0], cache_v[0], state_hgrn[0], attn_norm_g[0],
                  w_in[0], lb_all[0], hg_norm_g[0], da_q_norm_g[0], da_k_norm_g[0], da_lambda[0],
                  da_subln_g[0], w_out[0], ffn_norm_g[0], w_router[0], b_router[0], w_gu[0],
                  b_gu[0], w_down[0], b_down[0], lam_init)
    y_p, y_s, k_p, v_p, s_p, k_s, v_s, s_s = outs
    return (y_p, y_s, k_p[None], v_p[None], s_p[None], k_s[None], v_s[None], s_s[None])
```

```python
import functools
import math

import jax
import jax.numpy as jnp
from jax import lax
from jax.experimental import pallas as pl
from jax.experimental.pallas import tpu as pltpu

F32 = jnp.float32
BF16 = jnp.bfloat16
I32 = jnp.int32
U32 = jnp.uint32

RMS_EPS = 1e-6
CHUNK = 64
HG_HEADS = 8
HG_DIM = 128
DA_HEADS = 8
DA_QK_DIM = 64
DA_V_DIM = 128
HEAD_WIDTH = 1024
N_EXPERTS = 32
TOP_K = 4
SWIGLU_LIMIT = 7.0
SWIGLU_ALPHA = 1.702
LANES = 128
HG_TILE = 64
HG_SUB = 16
HG_GROUP = 4
EXP_CLAMP = 80.0
LOG2_E = 1.4426950408889634
V_COLS = 2 * DA_V_DIM
ATTN_SAFE_BOUND = 50.0
VMEM_LIMIT = 60 * 1024 * 1024
PROJ_TM, PROJ_TN = 256, 3584
DA_TM = 256
ATTN_TILE = 512
MIX_TM = 256
DISPATCH_TM = 256
MOE_BM, MOE_TH = 1024, 512
MOE_PARTS = 4
COMBINE_TM = 128


def _cparams(sem, vmem=VMEM_LIMIT):
    return pltpu.CompilerParams(dimension_semantics=sem, vmem_limit_bytes=vmem)


def _dot(a, b):
    return jnp.dot(a, b, preferred_element_type=F32)


def _dot_nt(a, b):
    return lax.dot_general(a, b, (((1,), (1,)), ((), ())), preferred_element_type=F32)


def _split_bf16(x):
    hi = x.astype(BF16)
    lo = (x - hi.astype(F32)).astype(BF16)
    return hi, lo


def _sigmoid(x):
    return 1.0 / (1.0 + jnp.exp(-x))


def _group_sum(sq, gmat):
    outs = []
    for c in range(sq.shape[1] // 256):
        hi, lo = _split_bf16(sq[:, c * 256:(c + 1) * 256])
        outs.append(_dot(hi, gmat) + _dot(lo, gmat))
    return jnp.concatenate(outs, axis=1)


def _proj_kernel(n_prompt_tiles, xp_ref, xs_ref, g_ref, w_ref, o_ref):
    is_prompt = pl.program_id(1) < n_prompt_tiles
    x = jnp.where(is_prompt, xp_ref[...], xs_ref[...])
    ms = jnp.mean(x * x, axis=-1, keepdims=True)
    h = ((x * lax.rsqrt(ms + RMS_EPS)) * g_ref[...]).astype(BF16)
    o_ref[...] = _dot(h, w_ref[...])


def _proj(x_p, x_s, g, w_bf16, tm, tn):
    d = x_p.shape[1]
    n = x_p.shape[0] + x_s.shape[0]
    npt = x_p.shape[0] // tm
    cols = w_bf16.shape[1]
    return pl.pallas_call(
        functools.partial(_proj_kernel, npt),
        grid=(cols // tn, n // tm),
        in_specs=[pl.BlockSpec((tm, d), lambda j, i: (jnp.minimum(i, npt - 1), 0)),
                  pl.BlockSpec((tm, d), lambda j, i: (jnp.maximum(i - npt, 0), 0)),
                  pl.BlockSpec((1, d), lambda j, i: (0, 0)),
                  pl.BlockSpec((d, tn), lambda j, i: (0, j))],
        out_specs=pl.BlockSpec((tm, tn), lambda j, i: (i, j)),
        out_shape=jax.ShapeDtypeStruct((n, cols), F32),
        compiler_params=_cparams(("arbitrary", "arbitrary")),
        name="proj",
    )(x_p, x_s, g, w_bf16)


def _hgrn_chunk(q_raw, f_raw, v, g_raw, lb, ng, states):
    q = q_raw * _sigmoid(q_raw)
    f = lb + (1.0 - lb) * _sigmoid(f_raw)
    logf = jnp.log(f)
    k = 1.0 - f
    gate = _sigmoid(g_raw)

    row = lax.broadcasted_iota(I32, (HG_TILE, HG_TILE), 0)
    col = lax.broadcasted_iota(I32, (HG_TILE, HG_TILE), 1)
    causal = col <= row
    tri = causal.astype(BF16)
    hi, lo = _split_bf16(logf)
    b = _dot(tri, hi) + _dot(tri, lo)
    b_last = b[HG_TILE - 1:HG_TILE, :]
    q_dec = q * jnp.exp(b)
    k_dec = k * jnp.exp(b_last - b)

    outs, new_states = [], []
    for h in range(HG_HEADS):
        sl = slice(h * HG_DIM, (h + 1) * HG_DIM)
        st = states[h]
        bh = b[:, sl]
        qh = q[:, sl]
        kh = k[:, sl]
        vh = v[:, sl].astype(BF16)
        o = _dot_nt(q_dec[:, sl].astype(BF16), st.astype(BF16))
        rows = []
        for i in range(HG_TILE // HG_SUB):
            r0 = i * HG_SUB
            base = bh[r0 - 1:r0, :] if i else jnp.zeros((1, HG_DIM), F32)
            qi = qh[r0:r0 + HG_SUB, :] * jnp.exp(bh[r0:r0 + HG_SUB, :] - base)
            ki = kh * jnp.exp(jnp.minimum(base - bh, EXP_CLAMP))
            rows.append(_dot_nt(qi.astype(BF16), ki.astype(BF16)))
        scores = jnp.where(causal, jnp.concatenate(rows, axis=0), 0.0)
        o = o + _dot(scores.astype(BF16), vh)
        new_states.append(st * jnp.exp(b_last[:, sl])
                          + _dot(v[:, sl].T.astype(BF16), k_dec[:, sl].astype(BF16)))
        ms = jnp.mean(o * o, axis=-1, keepdims=True)
        outs.append((o * lax.rsqrt(ms + RMS_EPS)) * ng)
    return jnp.concatenate(outs, axis=1) * gate, new_states


def _hgrn_kernel(n_prompt_steps, group, q_ref, f_ref, i_ref, g_ref, lb_ref, ng_ref, s0_ref,
                 o_ref, sout_ref, st_scr):
    t = pl.program_id(0)
    lb = lb_ref[...]
    ng = ng_ref[...]

    def chunk(g, states):
        rows = slice(g * HG_TILE, (g + 1) * HG_TILE)
        out, states = _hgrn_chunk(q_ref[rows, :], f_ref[rows, :], i_ref[rows, :], g_ref[rows, :],
                                  lb, ng, states)
        o_ref[rows, :] = out.astype(o_ref.dtype)
        return states

    @pl.when(t == 0)
    def _():
        st_scr[...] = jnp.zeros(st_scr.shape, F32)

    @pl.when(t < n_prompt_steps)
    def _():
        states = [st_scr[h] for h in range(HG_HEADS)]
        for g in range(group):
            states = chunk(g, states)
        for h in range(HG_HEADS):
            st_scr[h] = states[h]

    @pl.when(t == n_prompt_steps - 1)
    def _():
        for h in range(HG_HEADS):
            sout_ref[0, h] = st_scr[h].T
        if group > 1:
            sout_ref[1:group] = jnp.zeros((group - 1,) + tuple(sout_ref.shape[1:]), F32)

    @pl.when(t >= n_prompt_steps)
    def _():
        for g in range(group):
            states = chunk(g, [s0_ref[g, h].T for h in range(HG_HEADS)])
            for h in range(HG_HEADS):
                sout_ref[g, h] = states[h].T


def _hgrn(proj, lb, norm_g, state_s, n_prompt, group):
    n = proj.shape[0]
    rows = group * HG_TILE
    n_prompt_steps = n_prompt // rows
    s0_all = jnp.concatenate([jnp.zeros((group,) + state_s.shape[1:], F32), state_s], axis=0)
    seq = lambda t: (jnp.maximum(t - (n_prompt_steps - 1), 0), 0, 0, 0)
    col_spec = lambda c: pl.BlockSpec((rows, HEAD_WIDTH), lambda t, c=c: (t, c))
    return pl.pallas_call(
        functools.partial(_hgrn_kernel, n_prompt_steps, group),
        grid=(n // rows,),
        in_specs=[col_spec(0), col_spec(1), col_spec(2), col_spec(3),
                  pl.BlockSpec((1, HEAD_WIDTH), lambda t: (0, 0)),
                  pl.BlockSpec((1, HG_DIM), lambda t: (0, 0)),
                  pl.BlockSpec((group, HG_HEADS, HG_DIM, HG_DIM), seq)],
        out_specs=[pl.BlockSpec((rows, HEAD_WIDTH), lambda t: (t, 0)),
                   pl.BlockSpec((group, HG_HEADS, HG_DIM, HG_DIM), seq)],
        out_shape=[jax.ShapeDtypeStruct((n, HEAD_WIDTH), BF16),
                   jax.ShapeDtypeStruct(s0_all.shape, F32)],
        scratch_shapes=[pltpu.VMEM((HG_HEADS, HG_DIM, HG_DIM), F32)],
        compiler_params=_cparams(("arbitrary",)),
        name="hgrn",
    )(proj, proj, proj, proj, lb, norm_g, s0_all)


def _da_prep_kernel(n_prompt_tiles, dq_ref, dk_ref, dv_ref, gq_ref, gk_ref, gmat_ref,
                    qlo_ref, qhi_ref, kt_ref, vx_ref, kp_ref, ks_ref, vp_ref, vs_ref):
    i = pl.program_id(0)
    gmat = gmat_ref[...]
    inv = 1.0 / DA_QK_DIM

    dq = dq_ref[...]
    qn = dq * lax.rsqrt(_group_sum(dq * dq, gmat) * inv + RMS_EPS) * gq_ref[...]
    qn = qn * (DA_QK_DIM ** -0.5 * LOG2_E)
    lane = lax.broadcasted_iota(I32, qn.shape, 1)
    first = (lane & DA_QK_DIM) == 0
    qlo_ref[...] = jnp.where(first, qn, 0.0).astype(BF16)
    qhi_ref[...] = jnp.where(first, 0.0, qn).astype(BF16)

    dk = dk_ref[...]
    kn = dk * lax.rsqrt(_group_sum(dk * dk, gmat) * inv + RMS_EPS) * gk_ref[...]
    v = dv_ref[...]
    kt_ref[...] = kn.T.astype(BF16)
    ones_col = (lax.broadcasted_iota(I32, (v.shape[0], V_COLS - DA_V_DIM), 1) == 0).astype(F32)
    cols = []
    for h in range(DA_HEADS):
        cols += [v[:, h * DA_V_DIM:(h + 1) * DA_V_DIM], ones_col]
    vx_ref[...] = jnp.concatenate(cols, axis=1).astype(BF16)

    @pl.when(i < n_prompt_tiles)
    def _():
        kp_ref[...] = kn
        vp_ref[...] = v

    @pl.when(i >= n_prompt_tiles)
    def _():
        ks_ref[...] = kn
        vs_ref[...] = v


def _da_prep(proj, gq, gk, gmat, n_prompt, tm):
    n = proj.shape[0]
    npt = n_prompt // tm
    full = lambda: pl.BlockSpec((tm, HEAD_WIDTH), lambda i: (i, 0))
    p_spec = lambda: pl.BlockSpec((tm, HEAD_WIDTH), lambda i: (jnp.minimum(i, npt - 1), 0))
    s_spec = lambda: pl.BlockSpec((tm, HEAD_WIDTH), lambda i: (jnp.maximum(i - npt, 0), 0))
    vec = lambda: pl.BlockSpec((1, HEAD_WIDTH), lambda i: (0, 0))
    return pl.pallas_call(
        functools.partial(_da_prep_kernel, npt),
        grid=(n // tm,),
        in_specs=[pl.BlockSpec((tm, HEAD_WIDTH), lambda i: (i, 4)),
                  pl.BlockSpec((tm, HEAD_WIDTH), lambda i: (i, 5)),
                  pl.BlockSpec((tm, HEAD_WIDTH), lambda i: (i, 6)),
                  vec(), vec(),
                  pl.BlockSpec((256, 256), lambda i: (0, 0))],
        out_specs=[full(), full(),
                   pl.BlockSpec((HEAD_WIDTH, tm), lambda i: (0, i)),
                   pl.BlockSpec((tm, DA_HEADS * V_COLS), lambda i: (i, 0)),
                   p_spec(), s_spec(), p_spec(), s_spec()],
        out_shape=[jax.ShapeDtypeStruct((n, HEAD_WIDTH), BF16),
                   jax.ShapeDtypeStruct((n, HEAD_WIDTH), BF16),
                   jax.ShapeDtypeStruct((HEAD_WIDTH, n), BF16),
                   jax.ShapeDtypeStruct((n, DA_HEADS * V_COLS), BF16),
                   jax.ShapeDtypeStruct((n_prompt, HEAD_WIDTH), F32),
                   jax.ShapeDtypeStruct((n - n_prompt, HEAD_WIDTH), F32),
                   jax.ShapeDtypeStruct((n_prompt, HEAD_WIDTH), F32),
                   jax.ShapeDtypeStruct((n - n_prompt, HEAD_WIDTH), F32)],
        compiler_params=_cparams(("arbitrary",)),
        name="da_prep",
    )(proj, proj, proj, gq, gk, gmat)


def _lambda_of(lp, lam_init):
    a = jnp.sum(lp[0:1, :] * lp[1:2, :], axis=-1, keepdims=True)
    b = jnp.sum(lp[2:3, :] * lp[3:4, :], axis=-1, keepdims=True)
    return jnp.exp(a) - jnp.exp(b) + lam_init


def _subln(o, g, lam_init):
    ms = jnp.mean(o * o, axis=-1, keepdims=True)
    return (o * lax.rsqrt(ms + RMS_EPS)) * g * (1.0 - lam_init)


def _attn_q_kernel(lam_init, tile, qi_ref, kj_ref, bounded_ref, qlo_ref, qhi_ref, kt_ref, vx_ref,
                   bound_ref, lp_ref, sg_ref, o_ref, m_scr, acc_scr):
    s = pl.program_id(0)
    qi = qi_ref[s]
    kj = kj_ref[s]
    bounded = bounded_ref[0] != 0
    n_maps = 2 * DA_HEADS

    @pl.when(kj == 0)
    def _():
        m_scr[...] = jnp.full(m_scr.shape, -jnp.inf, F32)
        acc_scr[...] = jnp.zeros(acc_scr.shape, F32)

    def scores(r):
        h, c = divmod(r, 2)
        sl = slice(h * DA_V_DIM, (h + 1) * DA_V_DIM)
        return _dot((qlo_ref, qhi_ref)[c][:, sl], kt_ref[sl, :])

    def values(r):
        h = r // 2
        return vx_ref[:, h * V_COLS:(h + 1) * V_COLS]

    def sweep_bounded(visible):
        bound = bound_ref[0:1, 0:1]
        for r in range(n_maps):
            e = jnp.exp2(scores(r) - bound)
            if visible is not None:
                e = jnp.where(visible, e, 0.0)
            acc_scr[r] += _dot(e.astype(BF16), values(r))

    def sweep_online(visible):
        for r in range(n_maps):
            sc = scores(r)
            if visible is not None:
                sc = jnp.where(visible, sc, -jnp.inf)
            m_prev = m_scr[r]
            m_new = jnp.maximum(m_prev, jnp.max(sc, axis=-1, keepdims=True))
            alpha = jnp.exp2(m_prev - m_new)
            e = jnp.exp2(sc - m_new)
            acc_scr[r] = alpha * acc_scr[r] + _dot(e.astype(BF16), values(r))
            m_scr[r] = m_new

    def chunk_mask():
        shift = CHUNK.bit_length() - 1
        qry_chunk = lax.broadcasted_iota(I32, (tile, tile), 0) >> shift
        key_chunk = lax.broadcasted_iota(I32, (tile, tile), 1) >> shift
        return key_chunk <= qry_chunk

    off_diag = kj < qi
    on_diag = kj == qi
    unbounded = jnp.logical_not(bounded)

    @pl.when(jnp.logical_and(off_diag, bounded))
    def _():
        sweep_bounded(None)

    @pl.when(jnp.logical_and(on_diag, bounded))
    def _():
        sweep_bounded(chunk_mask())

    @pl.when(jnp.logical_and(off_diag, unbounded))
    def _():
        sweep_online(None)

    @pl.when(jnp.logical_and(on_diag, unbounded))
    def _():
        sweep_online(chunk_mask())

    @pl.when(on_diag)
    def _():
        lam = _lambda_of(lp_ref[...], lam_init)
        sg = sg_ref[...]
        outs = []
        for h in range(DA_HEADS):
            acc0 = acc_scr[2 * h]
            acc1 = acc_scr[2 * h + 1]
            a0 = acc0[:, 0:DA_V_DIM] / acc0[:, DA_V_DIM:DA_V_DIM + 1]
            a1 = acc1[:, 0:DA_V_DIM] / acc1[:, DA_V_DIM:DA_V_DIM + 1]
            outs.append(_subln(a0 - lam * a1, sg, lam_init))
        o_ref[...] = jnp.concatenate(outs, axis=1).astype(o_ref.dtype)


def _attn_prompt_q(qlo, qhi, kt, vx, bound, lp, sg, n_prompt, lam_init, tile):
    nq = n_prompt // tile
    qi_l, kj_l = [], []
    for i in range(nq):
        for j in range(i + 1):
            qi_l.append(i)
            kj_l.append(j)
    qi = jnp.asarray(qi_l, I32)
    kj = jnp.asarray(kj_l, I32)
    bounded = (bound <= ATTN_SAFE_BOUND).astype(I32).reshape(1)
    bound_row = jnp.broadcast_to(bound.astype(F32), (1, LANES))
    const = lambda shp: pl.BlockSpec(shp, lambda s, qi, kj, bd: (0, 0))
    grid_spec = pltpu.PrefetchScalarGridSpec(
        num_scalar_prefetch=3,
        grid=(len(qi_l),),
        in_specs=[pl.BlockSpec((tile, HEAD_WIDTH), lambda s, qi, kj, bd: (qi[s], 0)),
                  pl.BlockSpec((tile, HEAD_WIDTH), lambda s, qi, kj, bd: (qi[s], 0)),
                  pl.BlockSpec((HEAD_WIDTH, tile), lambda s, qi, kj, bd: (0, kj[s])),
                  pl.BlockSpec((tile, DA_HEADS * V_COLS), lambda s, qi, kj, bd: (kj[s], 0)),
                  const((1, LANES)), const((4, DA_QK_DIM)), const((1, DA_V_DIM))],
        out_specs=pl.BlockSpec((tile, HEAD_WIDTH), lambda s, qi, kj, bd: (qi[s], 0)),
        scratch_shapes=[pltpu.VMEM((2 * DA_HEADS, tile, 1), F32),
                        pltpu.VMEM((2 * DA_HEADS, tile, V_COLS), F32)])
    return pl.pallas_call(
        functools.partial(_attn_q_kernel, lam_init, tile),
        grid_spec=grid_spec,
        out_shape=jax.ShapeDtypeStruct((n_prompt, HEAD_WIDTH), BF16),
        compiler_params=_cparams(("arbitrary",)),
        name="attn_q",
    )(qi, kj, bounded, qlo, qhi, kt, vx, bound_row, lp, sg)


def _attn_s_kernel(lam_init, qlo_ref, qhi_ref, kn_ref, vn_ref, ckt_ref, cv_ref, lp_ref, sg_ref,
                   o_ref):
    past = ckt_ref.shape[-1]
    lam = _lambda_of(lp_ref[...], lam_init)
    sg = sg_ref[...]
    outs = []
    for h in range(DA_HEADS):
        sl = slice(h * DA_V_DIM, (h + 1) * DA_V_DIM)
        kn = kn_ref[:, sl].astype(BF16)
        vc = cv_ref[0, pl.ds(h, past, stride=DA_HEADS), :].astype(BF16)
        vn = vn_ref[:, sl].astype(BF16)
        acc = None
        for c, q_ref in enumerate((qlo_ref, qhi_ref)):
            qh = q_ref[:, sl]
            kct = ckt_ref[0, h, c].astype(BF16)
            s_c = _dot(qh[:, c * DA_QK_DIM:(c + 1) * DA_QK_DIM], kct)
            s_n = _dot_nt(qh, kn)
            m = jnp.maximum(jnp.max(s_c, axis=-1, keepdims=True),
                            jnp.max(s_n, axis=-1, keepdims=True))
            e_c = jnp.exp2(s_c - m)
            e_n = jnp.exp2(s_n - m)
            l = jnp.sum(e_c, axis=-1, keepdims=True) + jnp.sum(e_n, axis=-1, keepdims=True)
            part = (_dot(e_c.astype(BF16), vc) + _dot(e_n.astype(BF16), vn)) / l
            acc = part if c == 0 else acc - lam * part
        outs.append(_subln(acc, sg, lam_init))
    o_ref[...] = jnp.concatenate(outs, axis=1).astype(o_ref.dtype)


def _attn_sample(qlo, qhi, ks, vs, cache_k, cache_v, lp, sg, n_prompt, lam_init):
    bs, past = cache_k.shape[0], cache_k.shape[1]
    cache_kt = jnp.transpose(cache_k, (0, 2, 3, 4, 1))
    cache_v = cache_v.reshape(bs, past * DA_HEADS, DA_V_DIM)
    t = ks.shape[0] // bs
    off = n_prompt // t
    return pl.pallas_call(
        functools.partial(_attn_s_kernel, lam_init),
        grid=(bs,),
        in_specs=[pl.BlockSpec((t, HEAD_WIDTH), lambda b: (off + b, 0)),
                  pl.BlockSpec((t, HEAD_WIDTH), lambda b: (off + b, 0)),
                  pl.BlockSpec((t, HEAD_WIDTH), lambda b: (b, 0)),
                  pl.BlockSpec((t, HEAD_WIDTH), lambda b: (b, 0)),
                  pl.BlockSpec((1, DA_HEADS, 2, DA_QK_DIM, past), lambda b: (b, 0, 0, 0, 0)),
                  pl.BlockSpec((1, past * DA_HEADS, DA_V_DIM), lambda b: (b, 0, 0)),
                  pl.BlockSpec((4, DA_QK_DIM), lambda b: (0, 0)),
                  pl.BlockSpec((1, DA_V_DIM), lambda b: (0, 0))],
        out_specs=pl.BlockSpec((t, HEAD_WIDTH), lambda b: (b, 0)),
        out_shape=jax.ShapeDtypeStruct((bs * t, HEAD_WIDTH), BF16),
        compiler_params=_cparams(("arbitrary",)),
        name="attn_s",
    )(qlo, qhi, ks, vs, cache_kt, cache_v, lp, sg)


def _pack_bf16_pairs(x):
    half = x.shape[1] // 2
    lo = lax.bitcast_convert_type(x[:, :half].astype(BF16).astype(F32), U32)
    hi = lax.bitcast_convert_type(x[:, half:].astype(BF16).astype(F32), U32)
    return (lo >> 16) | (hi & jnp.uint32(0xFFFF0000))


def _unpack_bf16_pairs(w):
    lo = lax.bitcast_convert_type(w << 16, F32).astype(BF16)
    hi = lax.bitcast_convert_type(w & jnp.uint32(0xFFFF0000), F32).astype(BF16)
    return lo, hi


def _mix_kernel(n_prompt_tiles, hg_ref, dap_ref, das_ref, xp_ref, xs_ref, wo_ref, g2_ref, wr_ref,
                br_ref, x1_ref, h2p_ref, route_ref, gates_ref, cnt_ref, base_scr):
    i = pl.program_id(0)

    @pl.when(i == 0)
    def _():
        base_scr[...] = jnp.zeros(base_scr.shape, F32)

    tm = xp_ref.shape[0]
    is_prompt = i < n_prompt_tiles
    da = jnp.where(is_prompt, dap_ref[...], das_ref[...])
    x = jnp.where(is_prompt, xp_ref[...], xs_ref[...])
    x1 = x + _dot(hg_ref[...], wo_ref[0:HEAD_WIDTH, :]) \
        + _dot(da, wo_ref[HEAD_WIDTH:2 * HEAD_WIDTH, :])
    x1_ref[...] = x1
    ms = jnp.mean(x1 * x1, axis=-1, keepdims=True)
    h2 = (x1 * lax.rsqrt(ms + RMS_EPS)) * g2_ref[...]
    h2p_ref[...] = _pack_bf16_pairs(h2)

    h_hi, h_lo = _split_bf16(h2)
    w_hi, w_lo = _split_bf16(wr_ref[...])
    logits = _dot(h_hi, w_hi) + _dot(h_lo, w_hi) + _dot(h_hi, w_lo) + br_ref[...]
    lane = lax.broadcasted_iota(I32, logits.shape, 1)
    lanef = lane.astype(F32)
    lg = jnp.where(lane < N_EXPERTS, logits, -jnp.inf)

    sels, vals, idxs = [], [], []
    for _ in range(TOP_K):
        mx = jnp.max(lg, axis=-1, keepdims=True)
        idx = jnp.min(jnp.where(lg == mx, lanef, float(LANES)), axis=-1, keepdims=True)
        sel = lanef == idx
        lg = jnp.where(sel, -jnp.inf, lg)
        sels.append(sel)
        vals.append(mx)
        idxs.append(idx)
    es = [jnp.exp(v - vals[0]) for v in vals]
    den = es[0] + es[1] + es[2] + es[3]

    onehot = jnp.zeros(logits.shape, F32)
    for sel in sels:
        onehot = onehot + sel.astype(F32)
    row = lax.broadcasted_iota(I32, (tm, tm), 0)
    col = lax.broadcasted_iota(I32, (tm, tm), 1)
    strict = (col < row).astype(BF16)
    before = _dot(strict, onehot.astype(BF16)) + base_scr[0:1, :]
    base_scr[0:1, :] = base_scr[0:1, :] + jnp.sum(onehot, axis=0, keepdims=True)

    route = jnp.zeros(logits.shape, F32)
    gates = jnp.zeros(logits.shape, F32)
    for r in range(TOP_K):
        rank = jnp.sum(jnp.where(sels[r], before, 0.0), axis=-1, keepdims=True)
        route = jnp.where(lane == r, idxs[r], route)
        route = jnp.where(lane == TOP_K + r, rank, route)
        gates = jnp.where(lane == r, es[r] / den, gates)
    route_ref[...] = route.astype(I32)
    gates_ref[...] = gates
    cnt_ref[...] = jnp.broadcast_to(base_scr[0:1, :], cnt_ref.shape).astype(I32)


def _mix(hg, da_p, da_s, x_p, x_s, wo_bf16, g2, wr_pad, br_pad, tm):
    d = x_p.shape[1]
    n = x_p.shape[0] + x_s.shape[0]
    npt = x_p.shape[0] // tm
    row = lambda w: pl.BlockSpec((tm, w), lambda i: (i, 0))
    p_row = lambda w: pl.BlockSpec((tm, w), lambda i: (jnp.minimum(i, npt - 1), 0))
    s_row = lambda w: pl.BlockSpec((tm, w), lambda i: (jnp.maximum(i - npt, 0), 0))
    const = lambda shp: pl.BlockSpec(shp, lambda i: (0, 0))
    return pl.pallas_call(
        functools.partial(_mix_kernel, npt),
        grid=(n // tm,),
        in_specs=[row(HEAD_WIDTH), p_row(HEAD_WIDTH), s_row(HEAD_WIDTH), p_row(d), s_row(d),
                  const((2 * HEAD_WIDTH, d)), const((1, d)), const((d, LANES)), const((1, LANES))],
        out_specs=[row(d), row(d // 2), row(LANES), row(LANES), const((8, LANES))],
        out_shape=[jax.ShapeDtypeStruct((n, d), F32),
                   jax.ShapeDtypeStruct((n, d // 2), U32),
                   jax.ShapeDtypeStruct((n, LANES), I32),
                   jax.ShapeDtypeStruct((n, LANES), F32),
                   jax.ShapeDtypeStruct((8, LANES), I32)],
        scratch_shapes=[pltpu.VMEM((8, LANES), F32)],
        compiler_params=_cparams(("arbitrary",)),
        name="mix",
    )(hg, da_p, da_s, x_p, x_s, wo_bf16, g2, wr_pad, br_pad)


def _dispatch_copy(h_ref, xs_ref, sem, dest_ref, r, kk):
    return pltpu.make_async_copy(h_ref.at[pl.ds(r, 1)],
                                 xs_ref.at[pl.ds(dest_ref[r * TOP_K + kk], 1)], sem)


def _dispatch_kernel(zinfo_ref, dest_ref, h_ref, xs_ref, zbuf, sem, zsem):
    tm = h_ref.shape[0]
    bm = zbuf.shape[0]
    nb = xs_ref.shape[0] // bm

    def zero_copy(start):
        return pltpu.make_async_copy(zbuf, xs_ref.at[pl.ds(pl.multiple_of(start, bm), bm)], zsem)

    def for_each_zero_block(fn):
        for e in range(N_EXPERTS):
            start = zinfo_ref[e]

            @pl.when(start >= 0)
            def _():
                fn(zero_copy(start))

        def tail(b, carry):
            fn(zero_copy(b * bm))
            return carry

        lax.fori_loop(zinfo_ref[N_EXPERTS], nb, tail, 0)

    @pl.when(pl.program_id(0) == 0)
    def _():
        zbuf[...] = jnp.zeros(zbuf.shape, zbuf.dtype)
        for_each_zero_block(lambda cp: cp.start())
        for_each_zero_block(lambda cp: cp.wait())

    def issue(r, carry):
        for kk in range(TOP_K):
            _dispatch_copy(h_ref, xs_ref, sem, dest_ref, r, kk).start(priority=kk % 2)
        return carry

    def drain(r, carry):
        for kk in range(TOP_K):
            _dispatch_copy(h_ref, xs_ref, sem, dest_ref, r, kk).wait()
        return carry

    lax.fori_loop(0, tm, issue, 0)
    lax.fori_loop(0, tm, drain, 0)


def _dispatch(h2p, dest_flat, zinfo, n_rows, bm, tm):
    n, w = h2p.shape
    grid_spec = pltpu.PrefetchScalarGridSpec(
        num_scalar_prefetch=1,
        grid=(n // tm,),
        in_specs=[pl.BlockSpec((tm * TOP_K,), lambda i, z: (i,), memory_space=pltpu.SMEM),
                  pl.BlockSpec((tm, w), lambda i, z: (i, 0))],
        out_specs=pl.BlockSpec(memory_space=pl.ANY),
        scratch_shapes=[pltpu.VMEM((bm, w), h2p.dtype),
                        pltpu.SemaphoreType.DMA, pltpu.SemaphoreType.DMA])
    return pl.pallas_call(
        _dispatch_kernel,
        grid_spec=grid_spec,
        out_shape=jax.ShapeDtypeStruct((n_rows, w), h2p.dtype),
        compiler_params=_cparams(("arbitrary",)),
        name="dispatch",
    )(zinfo, dest_flat, h2p)


def _moe_kernel(n_parts, be_ref, nu_ref, parts_ref, xs_ref, wg_ref, wu_ref, bg_ref, bu_ref, wd_ref,
                bd_ref, o_ref, xb_scr):
    i = pl.program_id(0)
    j = pl.program_id(1)
    parts = parts_ref[i]
    used = parts > 0
    part_rows = o_ref.shape[0] // n_parts

    @pl.when(jnp.logical_and(j == 0, used))
    def _():
        half = xs_ref.shape[1]
        lo, hi = _unpack_bf16_pairs(xs_ref[...])
        xb_scr[:, 0:half] = lo
        xb_scr[:, half:2 * half] = hi
        o_ref[...] = jnp.broadcast_to(bd_ref[0], o_ref.shape)

    @pl.when(jnp.logical_and(j == 0, jnp.logical_not(used)))
    def _():
        o_ref[...] = jnp.zeros(o_ref.shape, F32)

    def experts_on(rows):
        xb = xb_scr[0:rows, :]
        gate = _dot(xb, wg_ref[0].astype(BF16)) + bg_ref[0]
        up = _dot(xb, wu_ref[0].astype(BF16)) + bu_ref[0]
        gate = jnp.minimum(gate, SWIGLU_LIMIT)
        up = jnp.clip(up, -SWIGLU_LIMIT, SWIGLU_LIMIT)
        glu = gate * _sigmoid(SWIGLU_ALPHA * gate)
        mid = ((up + 1.0) * glu).astype(BF16)
        o_ref[0:rows, :] += _dot(mid, wd_ref[0].astype(BF16))

    for p in range(1, n_parts + 1):
        @pl.when(parts == p)
        def _(p=p):
            experts_on(p * part_rows)


def _moe(xs, block_expert, n_used, block_parts, n_parts, w_gu, b_gu, w_down, b_down, bm, th):
    rows, half = xs.shape
    d = 2 * half
    n_e, _, two_h = w_gu.shape
    hid = two_h // 2
    nj = hid // th
    nb = rows // bm
    b_gu3 = b_gu.reshape(n_e, 1, two_h)
    b_dn3 = b_down.reshape(n_e, 1, d)

    def xrow(i, j, be, nu, bp):
        return (jnp.minimum(i, jnp.maximum(nu[0] - 1, 0)), 0)

    def hcol(i, j, nu):
        return jnp.where(i < nu[0], j, nj - 1)

    grid_spec = pltpu.PrefetchScalarGridSpec(
        num_scalar_prefetch=3,
        grid=(nb, nj),
        in_specs=[pl.BlockSpec((bm, half), xrow),
                  pl.BlockSpec((1, d, th), lambda i, j, be, nu, bp: (be[i], 0, hcol(i, j, nu))),
                  pl.BlockSpec((1, d, th),
                               lambda i, j, be, nu, bp: (be[i], 0, nj + hcol(i, j, nu))),
                  pl.BlockSpec((1, 1, th), lambda i, j, be, nu, bp: (be[i], 0, hcol(i, j, nu))),
                  pl.BlockSpec((1, 1, th),
                               lambda i, j, be, nu, bp: (be[i], 0, nj + hcol(i, j, nu))),
                  pl.BlockSpec((1, th, d), lambda i, j, be, nu, bp: (be[i], hcol(i, j, nu), 0)),
                  pl.BlockSpec((1, 1, d), lambda i, j, be, nu, bp: (be[i], 0, 0))],
        out_specs=pl.BlockSpec((bm, d), lambda i, j, be, nu, bp: (i, 0)),
        scratch_shapes=[pltpu.VMEM((bm, d), BF16)])
    return pl.pallas_call(
        functools.partial(_moe_kernel, n_parts),
        grid_spec=grid_spec,
        out_shape=jax.ShapeDtypeStruct((rows, d), F32),
        compiler_params=_cparams(("arbitrary", "arbitrary")),
        name="moe",
    )(block_expert, n_used, block_parts, xs, w_gu, w_gu, b_gu3, b_gu3, w_down, b_dn3)


def _combine_kernel(n_prompt_tiles, n_tiles, dest_ref, dest_next_ref, x1_ref, gates_ref, yb_ref,
                    yp_ref, ys_ref, buf, sem):
    i = pl.program_id(0)
    tm = x1_ref.shape[0]
    slot = i % 2

    def for_each_row_copy(d_ref, s, fn):
        def body(r, carry):
            for kk in range(TOP_K):
                fn(pltpu.make_async_copy(yb_ref.at[pl.ds(d_ref[r * TOP_K + kk], 1)],
                                         buf.at[s, kk, pl.ds(r, 1)], sem.at[s]), kk)
            return carry

        lax.fori_loop(0, tm, body, 0)

    start = lambda cp, kk: cp.start(priority=kk % 2)

    @pl.when(i == 0)
    def _():
        for_each_row_copy(dest_ref, 0, start)

    @pl.when(i + 1 < n_tiles)
    def _():
        for_each_row_copy(dest_next_ref, 1 - slot, start)

    for_each_row_copy(dest_ref, slot, lambda cp, kk: cp.wait())

    g = gates_ref[...]
    y = x1_ref[...]
    for kk in range(TOP_K):
        y = y + g[:, kk:kk + 1] * buf[slot, kk]

    @pl.when(i < n_prompt_tiles)
    def _():
        yp_ref[...] = y

    @pl.when(i >= n_prompt_tiles)
    def _():
        ys_ref[...] = y


def _combine(x1, gates, dest_flat, yb, n_prompt, tm):
    n, d = x1.shape
    npt = n_prompt // tm
    n_tiles = n // tm
    return pl.pallas_call(
        functools.partial(_combine_kernel, npt, n_tiles),
        grid=(n_tiles,),
        in_specs=[pl.BlockSpec((tm * TOP_K,), lambda i: (i,), memory_space=pltpu.SMEM),
                  pl.BlockSpec((tm * TOP_K,), lambda i: (jnp.minimum(i + 1, n_tiles - 1),),
                               memory_space=pltpu.SMEM),
                  pl.BlockSpec((tm, d), lambda i: (i, 0)),
                  pl.BlockSpec((tm, LANES), lambda i: (i, 0)),
                  pl.BlockSpec(memory_space=pl.ANY)],
        out_specs=[pl.BlockSpec((tm, d), lambda i: (jnp.minimum(i, npt - 1), 0)),
                   pl.BlockSpec((tm, d), lambda i: (jnp.maximum(i - npt, 0), 0))],
        out_shape=[jax.ShapeDtypeStruct((n_prompt, d), F32),
                   jax.ShapeDtypeStruct((n - n_prompt, d), F32)],
        scratch_shapes=[pltpu.VMEM((2, TOP_K, tm, d), F32), pltpu.SemaphoreType.DMA((2,))],
        compiler_params=_cparams(("arbitrary",)),
        name="combine",
    )(dest_flat, dest_flat, x1, gates, yb)


def _group_matrix(group):
    r = jnp.arange(256) // group
    return (r[:, None] == r[None, :]).astype(BF16)


def _tile(n, want):
    t = min(want, n)
    while n % t:
        t //= 2
    return t


def _layer(x_prompt, x_sample, cache_k, cache_v, state_hgrn, attn_norm_g, w_in, lb, hg_norm_g,
           da_q_norm_g, da_k_norm_g, da_lambda, da_subln_g, w_out, ffn_norm_g, w_router,
           b_router, w_gu, b_gu, w_down, b_down, lam_init):
    bp, s, d = x_prompt.shape
    bs, t, _ = x_sample.shape
    assert bp == 1 and t == HG_TILE and s % HG_TILE == 0
    n_p = bp * s
    n = n_p + bs * t
    x_p = x_prompt.reshape(n_p, d)
    x_s = x_sample.reshape(bs * t, d)
    row_gcd = math.gcd(n_p, bs * t)

    proj = _proj(x_p, x_s, attn_norm_g.reshape(1, d), w_in.astype(BF16), _tile(row_gcd, PROJ_TM),
                 PROJ_TN)

    hg_group = _tile(math.gcd(n_p // HG_TILE, bs), HG_GROUP)
    hg, s_all = _hgrn(proj, lb.reshape(1, HEAD_WIDTH), hg_norm_g.reshape(1, HG_DIM), state_hgrn,
                      n_p, hg_group)

    tm_da = _tile(row_gcd, DA_TM)
    gq = jnp.tile(da_q_norm_g, 2 * DA_HEADS).reshape(1, HEAD_WIDTH)
    gk = jnp.tile(da_k_norm_g, 2 * DA_HEADS).reshape(1, HEAD_WIDTH)
    qlo, qhi, kt, vx, k_p, k_s, v_p, v_s = _da_prep(proj, gq, gk, _group_matrix(DA_QK_DIM), n_p,
                                                    tm_da)
    sg = da_subln_g.reshape(1, DA_V_DIM)
    score_bound = (DA_QK_DIM ** 0.5 * LOG2_E) * jnp.max(jnp.abs(da_q_norm_g)) \
        * jnp.max(jnp.abs(da_k_norm_g))
    da_p = _attn_prompt_q(qlo, qhi, kt, vx, score_bound, da_lambda, sg, n_p, lam_init,
                          _tile(n_p, ATTN_TILE))
    past = cache_k.shape[1]
    da_s = _attn_sample(qlo, qhi, k_s, v_s, cache_k, cache_v, da_lambda, sg, n_p, lam_init)

    wr_pad = jnp.pad(w_router, ((0, 0), (0, LANES - N_EXPERTS)))
    br_pad = jnp.pad(b_router, (0, LANES - N_EXPERTS)).reshape(1, LANES)
    x1, h2p, route, gates, cnt = _mix(hg, da_p, da_s, x_p, x_s, w_out.astype(BF16),
                                      ffn_norm_g.reshape(1, d), wr_pad, br_pad,
                                      _tile(row_gcd, MIX_TM))

    bm = MOE_BM
    counts = cnt[0, :N_EXPERTS]
    padded = (counts + bm - 1) // bm * bm
    pend = jnp.cumsum(padded)
    pstart = pend - padded
    dest = (pstart[route[:, 0:TOP_K]] + route[:, TOP_K:2 * TOP_K]).reshape(-1).astype(I32)
    nb = (n * TOP_K + N_EXPERTS * (bm - 1)) // bm
    n_used = (pend[-1] // bm).astype(I32)
    blk = jnp.minimum(jnp.arange(nb, dtype=I32), jnp.maximum(n_used - 1, 0))
    block_expert = jnp.minimum(jnp.sum(pend[None, :] <= (blk * bm)[:, None], axis=1),
                               N_EXPERTS - 1).astype(I32)
    zinfo = jnp.concatenate([jnp.where(padded > 0, pend - bm, -1), n_used[None]]).astype(I32)
    n_parts = MOE_PARTS if bm % (8 * MOE_PARTS) == 0 else 1
    blk_all = jnp.arange(nb, dtype=I32)
    valid_rows = jnp.clip(counts[block_expert] - (blk_all * bm - pstart[block_expert]), 0, bm)
    part_rows = bm // n_parts
    block_parts = jnp.where(blk_all < n_used, (valid_rows + part_rows - 1) // part_rows,
                            0).astype(I32)

    xs = _dispatch(h2p, dest, zinfo, nb * bm, bm, _tile(n, DISPATCH_TM))
    yb = _moe(xs, block_expert, n_used.reshape(1), block_parts, n_parts, w_gu, b_gu, w_down,
              b_down, bm, min(MOE_TH, w_down.shape[1]))
    y_p, y_s = _combine(x1, gates, dest, yb, n_p, _tile(row_gcd, COMBINE_TM))

    return (y_p.reshape(bp, s, d), y_s.reshape(bs, t, d),
            k_p.reshape(bp, s, DA_HEADS, 2, DA_QK_DIM), v_p.reshape(bp, s, DA_HEADS, DA_V_DIM),
            s_all[0:1],
            k_s.reshape(bs, t, DA_HEADS, 2, DA_QK_DIM), v_s.reshape(bs, t, DA_HEADS, DA_V_DIM),
            s_all[hg_group:])


def kernel(x_prompt, x_sample, cache_k, cache_v, state_hgrn, attn_norm_g, w_in, hg_lb_logits,
           hg_norm_g, da_q_norm_g, da_k_norm_g, da_lambda, da_subln_g, w_out, ffn_norm_g,
           w_router, b_router, w_gu, b_gu, w_down, b_down):
    depth = w_in.shape[0]
    assert depth == 1, "single-layer step"
    lb_all = jnp.cumsum(jax.nn.softmax(hg_lb_logits.astype(F32), axis=0), axis=0)
    lam_init = 0.8 - 0.6 * math.exp(-0.3 * 0)
    outs = _layer(x_prompt, x_sample, cache_k[0], cache_v[0], state_hgrn[0], attn_norm_g[0],
                  w_in[0], lb_all[0], hg_norm_g[0], da_q_norm_g[0], da_k_norm_g[0], da_lambda[0],
                  da_subln_g[0], w_out[0], ffn_norm_g[0], w_router[0], b_router[0], w_gu[0],
                  b_gu[0], w_down[0], b_down[0], lam_init)
    y_p, y_s, k_p, v_p, s_p, k_s, v_s, s_s = outs
    return (y_p, y_s, k_p[None], v_p[None], s_p[None], k_s[None], v_s[None], s_s[None])
```
